```python
import math
import jax, jax.numpy as jnp
from jax import lax
import numpy as np

D_MODEL = 4096
BATCH = 1
SEQ = 16384
DEPTH = 2

N_MIXERS = 2

GDN_QK_HEADS = 32
GDN_V_HEADS = 64
GDN_HEAD_DIM = 128
GDN_KEY_DIM = GDN_QK_HEADS * GDN_HEAD_DIM
GDN_VALUE_DIM = GDN_V_HEADS * GDN_HEAD_DIM
GDN_CONV_DIM = 2 * GDN_KEY_DIM + GDN_VALUE_DIM
GDN_IN_DIM = GDN_CONV_DIM + GDN_VALUE_DIM + 2 * GDN_V_HEADS
CONV_WIDTH = 4
GDN_CHUNK = 64

HGRN_EXPAND = 128
HGRN_HEADS = D_MODEL // HGRN_EXPAND
HGRN_HEAD_DIM = D_MODEL // HGRN_HEADS
HGRN_FORGET_DIM = HGRN_HEADS * HGRN_EXPAND
HGRN_VALUE_DIM = HGRN_HEADS * HGRN_HEAD_DIM
HGRN_IN_DIM = 2 * HGRN_FORGET_DIM + 2 * HGRN_VALUE_DIM
HGRN_CHUNK = 64

MOE_GROUPS = 8
MOE_EXPERTS_PER_GROUP = 8
MOE_N_EXPERTS = MOE_GROUPS * MOE_EXPERTS_PER_GROUP
MOE_TOP_K = 2
MOE_D_FF = 256
MOE_BLOCK = 128

NORM_EPS = 1e-6
DEEPNORM_ALPHA = (2 * DEPTH) ** 0.25
DEEPNORM_BETA = (8 * DEPTH) ** -0.25
N_GDN_LAYERS = (DEPTH + 1) // 2
N_HGRN_LAYERS = DEPTH // 2

kernel_name = 'hybrid_gdn_hgrn2_hmoe_deepnorm'


def _rmsnorm(x, w):
    x = x.astype(jnp.float32)
    return x * lax.rsqrt(jnp.mean(x * x, axis=-1, keepdims=True) + NORM_EPS) * w.astype(jnp.float32)


def _layernorm(x, w, b):
    xf = x.astype(jnp.float32)
    mu = jnp.mean(xf, axis=-1, keepdims=True)
    var = jnp.mean(jnp.square(xf - mu), axis=-1, keepdims=True)
    y = (xf - mu) * lax.rsqrt(var + NORM_EPS) * w.astype(jnp.float32) + b.astype(jnp.float32)
    return y.astype(x.dtype)


def _l2norm(x):
    return x * lax.rsqrt(jnp.sum(x * x, axis=-1, keepdims=True) + NORM_EPS)


def _chunk_major(t, chunk):
    b, s, h = t.shape[:3]
    rest = t.shape[3:]
    t = t.reshape(b, s // chunk, chunk, h, *rest)
    return jnp.transpose(t, (1, 0, 3, 2) + tuple(range(4, t.ndim)))


def _seq_major(t):
    n, b, h, c = t.shape[:4]
    rest = t.shape[4:]
    t = jnp.transpose(t, (1, 0, 3, 2) + tuple(range(4, t.ndim)))
    return t.reshape(b, n * c, h, *rest)


def _causal_depthwise_conv(x, w):
    ch = x.shape[-1]
    return lax.conv_general_dilated(
        x, w[:, None, :].astype(x.dtype), window_strides=(1,),
        padding=[(CONV_WIDTH - 1, 0)], dimension_numbers=('NWC', 'WIO', 'NWC'),
        feature_group_count=ch)


def _chunk_gated_delta_rule(q, k, v, g, beta):
    C = GDN_CHUNK
    qc, kc, vc = _chunk_major(q, C), _chunk_major(k, C), _chunk_major(v, C)
    gc, bc = _chunk_major(g, C), _chunk_major(beta, C)
    dv = v.shape[-1]
    gcum = jnp.cumsum(gc, axis=-1)
    causal = jnp.tril(jnp.ones((C, C), dtype=bool))
    strict = jnp.tril(jnp.ones((C, C), dtype=bool), k=-1)
    decay = jnp.exp(jnp.where(causal, gcum[..., :, None] - gcum[..., None, :], -jnp.inf))
    kb = kc * bc[..., None]
    a_strict = jnp.where(strict, jnp.einsum('nbhid,nbhjd->nbhij', kb, kc) * decay, 0.0)
    t_mat = a_strict + jnp.eye(C, dtype=jnp.float32)
    rhs = jnp.concatenate([vc * bc[..., None], kb * jnp.exp(gcum)[..., None]], axis=-1)
    sol = lax.linalg.triangular_solve(t_mat, rhs, left_side=True, lower=True, unit_diagonal=True)
    u_c, w_c = sol[..., :dv], sol[..., dv:]
    qk = jnp.where(causal, jnp.einsum('nbhid,nbhjd->nbhij', qc, kc) * decay, 0.0)
    q_dec = qc * jnp.exp(gcum)[..., None]
    k_dec = kc * jnp.exp(gcum[..., -1:] - gcum)[..., None]
    g_tot = jnp.exp(gcum[..., -1])

    def step(state, xs):
        qk_n, qd_n, kd_n, u_n, w_n, gt_n = xs
        v_new = u_n - jnp.einsum('bhcd,bhde->bhce', w_n, state)
        o_n = jnp.einsum('bhcd,bhde->bhce', qd_n, state) + jnp.einsum('bhij,bhje->bhie', qk_n, v_new)
        state = state * gt_n[..., None, None] + jnp.einsum('bhcd,bhce->bhde', kd_n, v_new)
        return state, o_n

    b, h, dk = q.shape[0], q.shape[2], q.shape[3]
    state0 = jnp.zeros((b, h, dk, dv), jnp.float32)
    _, o = lax.scan(step, state0, (qk, q_dec, k_dec, u_c, w_c, g_tot))
    return _seq_major(o)


def _chunk_gla(q, k, v, log_f):
    C = HGRN_CHUNK
    qc, kc, vc, gc = (_chunk_major(t, C) for t in (q, k, v, log_f))
    bcum = jnp.cumsum(gc, axis=3)
    causal = jnp.tril(jnp.ones((C, C), dtype=bool))[:, :, None]

    def step(state, xs):
        q_n, k_n, v_n, b_n = xs
        dec = jnp.exp(jnp.where(causal, b_n[..., :, None, :] - b_n[..., None, :, :], -jnp.inf))
        att = jnp.einsum('bhid,bhijd,bhjd->bhij', q_n, dec, k_n)
        o_n = (jnp.einsum('bhij,bhje->bhie', att, v_n)
               + jnp.einsum('bhid,bhde->bhie', q_n * jnp.exp(b_n), state))
        b_last = b_n[..., -1:, :]
        state = (state * jnp.exp(b_last)[..., 0, :, None]
                 + jnp.einsum('bhcd,bhce->bhde', k_n * jnp.exp(b_last - b_n), v_n))
        return state, o_n

    b, h, dk, dv = q.shape[0], q.shape[2], q.shape[3], v.shape[3]
    state0 = jnp.zeros((b, h, dk, dv), jnp.float32)
    _, o = lax.scan(step, state0, (qc, kc, vc, bcum))
    return _seq_major(o)


def gated_deltanet_mixer(u, w_in, conv_w, a_log, dt_bias, norm_w, w_out):
    B, S, _ = u.shape
    proj = u @ w_in
    qkv, z, a, b = jnp.split(proj, [GDN_CONV_DIM, GDN_CONV_DIM + GDN_VALUE_DIM,
                                    GDN_CONV_DIM + GDN_VALUE_DIM + GDN_V_HEADS], axis=-1)
    qkv = jax.nn.silu(_causal_depthwise_conv(qkv, conv_w)).astype(jnp.float32)
    q, k, v = jnp.split(qkv, [GDN_KEY_DIM, 2 * GDN_KEY_DIM], axis=-1)
    rep = GDN_V_HEADS // GDN_QK_HEADS
    q = jnp.repeat(_l2norm(q.reshape(B, S, GDN_QK_HEADS, GDN_HEAD_DIM)), rep, axis=2) * (GDN_HEAD_DIM ** -0.5)
    k = jnp.repeat(_l2norm(k.reshape(B, S, GDN_QK_HEADS, GDN_HEAD_DIM)), rep, axis=2)
    v = v.reshape(B, S, GDN_V_HEADS, GDN_HEAD_DIM)
    beta = jax.nn.sigmoid(b.astype(jnp.float32))
    g = -jnp.exp(a_log.astype(jnp.float32)) * jax.nn.softplus(a.astype(jnp.float32) + dt_bias.astype(jnp.float32))
    o = _chunk_gated_delta_rule(q, k, v, g, beta)
    o = _rmsnorm(o, norm_w) * jax.nn.silu(z.astype(jnp.float32).reshape(B, S, GDN_V_HEADS, GDN_HEAD_DIM))
    return o.reshape(B, S, GDN_VALUE_DIM).astype(u.dtype) @ w_out


def hgrn2_mixer(u, w_in, lower_bound, norm_w, w_out):
    B, S, _ = u.shape
    proj = (u @ w_in).astype(jnp.float32)
    q, f, i, og = jnp.split(proj, [HGRN_FORGET_DIM, 2 * HGRN_FORGET_DIM,
                                   2 * HGRN_FORGET_DIM + HGRN_VALUE_DIM], axis=-1)
    lb = lower_bound.astype(jnp.float32)
    log_f = jnp.logaddexp(jnp.log(lb), jnp.log1p(-lb) + jax.nn.log_sigmoid(f))
    k = (1.0 - lb) * jax.nn.sigmoid(-f)
    hd = lambda t, d: t.reshape(B, S, HGRN_HEADS, d)
    o = _chunk_gla(hd(jax.nn.silu(q), HGRN_EXPAND), hd(k, HGRN_EXPAND),
                   hd(i, HGRN_HEAD_DIM), hd(log_f, HGRN_EXPAND))
    o = _rmsnorm(o, norm_w) * jax.nn.silu(hd(og, HGRN_HEAD_DIM))
    return o.reshape(B, S, HGRN_VALUE_DIM).astype(u.dtype) @ w_out


def hierarchical_moe(u, w_group, w_expert, w_gate, w_up, w_down):
    B, S, D = u.shape
    xt = u.reshape(-1, D)
    T = xt.shape[0]
    xf = xt.astype(jnp.float32)
    p_group = jax.nn.softmax(xf @ w_group.astype(jnp.float32), axis=-1)
    grp = jnp.argmax(p_group, axis=-1)
    p_grp = jnp.take_along_axis(p_group, grp[:, None], axis=1)
    logits_e = (xf @ w_expert.astype(jnp.float32)).reshape(T, MOE_GROUPS, MOE_EXPERTS_PER_GROUP)
    logits_sel = jnp.take_along_axis(logits_e, grp[:, None, None], axis=1)[:, 0]
    top_p, top_i = lax.top_k(jax.nn.softmax(logits_sel, axis=-1), MOE_TOP_K)
    gate = p_grp * top_p / jnp.sum(top_p, axis=-1, keepdims=True)
    eid = (grp[:, None] * MOE_EXPERTS_PER_GROUP + top_i).reshape(-1)
    tok = jnp.repeat(jnp.arange(T, dtype=jnp.int32), MOE_TOP_K)
    wts = gate.reshape(-1)
    A = T * MOE_TOP_K
    order = jnp.argsort(eid)
    eid_s, tok_s, w_s = eid[order], tok[order], wts[order]
    counts = jnp.bincount(eid, length=MOE_N_EXPERTS)
    padded = (counts + MOE_BLOCK - 1) // MOE_BLOCK * MOE_BLOCK
    pad_end = jnp.cumsum(padded)
    pad_start = pad_end - padded
    start = jnp.cumsum(counts) - counts
    dest = pad_start[eid_s] + (jnp.arange(A, dtype=jnp.int32) - start[eid_s])
    n_blk = -(-A // MOE_BLOCK) + MOE_N_EXPERTS
    P = n_blk * MOE_BLOCK
    buf_tok = jnp.zeros((P,), jnp.int32).at[dest].set(tok_s)
    buf_w = jnp.zeros((P,), jnp.float32).at[dest].set(w_s)
    blk_e = jnp.minimum(jnp.searchsorted(pad_end, jnp.arange(n_blk) * MOE_BLOCK, side='right'),
                        MOE_N_EXPERTS - 1)

    def step(out, blk):
        idx, wb, e = blk
        xb = xt[idx]
        h = jax.nn.silu(xb @ w_gate[e]) * (xb @ w_up[e])
        yb = (h @ w_down[e]) * wb[:, None].astype(h.dtype)
        return out.at[idx].add(yb), None

    out, _ = lax.scan(step, jnp.zeros_like(xt),
                      (buf_tok.reshape(n_blk, MOE_BLOCK), buf_w.reshape(n_blk, MOE_BLOCK), blk_e))
    return out.reshape(B, S, D)


def setup_inputs(seed: int = 0) -> dict:
    key = jax.random.key(seed)
    ks = jax.random.split(key, 24)
    nrm = jax.random.normal
    D = D_MODEL
    x = nrm(ks[0], (BATCH, SEQ, D), jnp.float32)
    c = nrm(ks[1], (BATCH, D), jnp.float32)
    ada_w = nrm(ks[2], (DEPTH, D, 6 * D), jnp.float32) * (0.5 * D ** -0.5)
    ada_b = 0.01 * nrm(ks[3], (DEPTH, 6 * D), jnp.float32)
    ln_w = 1.0 + 0.02 * nrm(ks[4], (DEPTH, 2, D), jnp.float32)
    ln_b = 0.02 * nrm(ks[5], (DEPTH, 2, D), jnp.float32)
    gdn_w_in = nrm(ks[6], (N_GDN_LAYERS, D, GDN_IN_DIM), jnp.float32) * (D ** -0.5)
    gdn_conv_w = nrm(ks[7], (N_GDN_LAYERS, CONV_WIDTH, GDN_CONV_DIM), jnp.float32) * (CONV_WIDTH ** -0.5)
    gdn_a_log = jnp.log(jax.random.uniform(ks[8], (N_GDN_LAYERS, GDN_V_HEADS), jnp.float32, 1.0, 16.0))
    dt = jnp.exp(jax.random.uniform(ks[9], (N_GDN_LAYERS, GDN_V_HEADS), jnp.float32,
                                    math.log(1e-3), math.log(1e-1)))
    gdn_dt_bias = dt + jnp.log(-jnp.expm1(-dt))
    gdn_norm_w = 1.0 + 0.02 * nrm(ks[10], (N_GDN_LAYERS, GDN_HEAD_DIM), jnp.float32)
    gdn_w_out = nrm(ks[11], (N_GDN_LAYERS, GDN_VALUE_DIM, D), jnp.float32) * (GDN_VALUE_DIM ** -0.5 * DEEPNORM_BETA)
    hgrn_w_in = nrm(ks[12], (N_HGRN_LAYERS, D, HGRN_IN_DIM), jnp.float32) * (D ** -0.5)
    hgrn_lb_logits = 0.5 * nrm(ks[13], (DEPTH, HGRN_FORGET_DIM), jnp.float32)
    hgrn_norm_w = 1.0 + 0.02 * nrm(ks[14], (N_HGRN_LAYERS, HGRN_HEAD_DIM), jnp.float32)
    hgrn_w_out = nrm(ks[15], (N_HGRN_LAYERS, HGRN_VALUE_DIM, D), jnp.float32) * (HGRN_VALUE_DIM ** -0.5 * DEEPNORM_BETA)
    moe_w_group = nrm(ks[16], (DEPTH, D, MOE_GROUPS), jnp.float32) * (D ** -0.5)
    moe_w_expert = nrm(ks[17], (DEPTH, D, MOE_N_EXPERTS), jnp.float32) * (D ** -0.5)
    moe_w_gate = nrm(ks[18], (DEPTH, MOE_N_EXPERTS, D, MOE_D_FF), jnp.float32) * (D ** -0.5)
    moe_w_up = nrm(ks[19], (DEPTH, MOE_N_EXPERTS, D, MOE_D_FF), jnp.float32) * (D ** -0.5)
    moe_w_down = nrm(ks[20], (DEPTH, MOE_N_EXPERTS, MOE_D_FF, D), jnp.float32) * (MOE_D_FF ** -0.5 * DEEPNORM_BETA)
    return {'x': x, 'c': c, 'ada_w': ada_w, 'ada_b': ada_b, 'ln_w': ln_w, 'ln_b': ln_b,
            'gdn_w_in': gdn_w_in, 'gdn_conv_w': gdn_conv_w, 'gdn_a_log': gdn_a_log,
            'gdn_dt_bias': gdn_dt_bias, 'gdn_norm_w': gdn_norm_w, 'gdn_w_out': gdn_w_out,
            'hgrn_w_in': hgrn_w_in, 'hgrn_lb_logits': hgrn_lb_logits, 'hgrn_norm_w': hgrn_norm_w,
            'hgrn_w_out': hgrn_w_out, 'moe_w_group': moe_w_group, 'moe_w_expert': moe_w_expert,
            'moe_w_gate': moe_w_gate, 'moe_w_up': moe_w_up, 'moe_w_down': moe_w_down}


def reference(x, c, ada_w, ada_b, ln_w, ln_b, gdn_w_in, gdn_conv_w, gdn_a_log, gdn_dt_bias,
              gdn_norm_w, gdn_w_out, hgrn_w_in, hgrn_lb_logits, hgrn_norm_w, hgrn_w_out,
              moe_w_group, moe_w_expert, moe_w_gate, moe_w_up, moe_w_down):
    p_lb = jax.nn.softmax(hgrn_lb_logits.astype(jnp.float32), axis=0)
    lower_bounds = jnp.cumsum(p_lb, axis=0) - p_lb[0]
    cond = jax.nn.silu(c)
    h = x
    for layer in range(DEPTH):
        mod = cond @ ada_w[layer] + ada_b[layer]
        shift1, scale1, gate1, shift2, scale2, gate2 = [m[:, None, :] for m in jnp.split(mod, 6, axis=-1)]
        u = h * (1.0 + scale1) + shift1
        j = layer // N_MIXERS
        if layer % N_MIXERS == 0:
            y = gated_deltanet_mixer(u, gdn_w_in[j], gdn_conv_w[j], gdn_a_log[j], gdn_dt_bias[j],
                                     gdn_norm_w[j], gdn_w_out[j])
        else:
            y = hgrn2_mixer(u, hgrn_w_in[j], lower_bounds[layer], hgrn_norm_w[j], hgrn_w_out[j])
        h = _layernorm(DEEPNORM_ALPHA * h + (1.0 + gate1) * y, ln_w[layer, 0], ln_b[layer, 0])
        u = h * (1.0 + scale2) + shift2
        y = hierarchical_moe(u, moe_w_group[layer], moe_w_expert[layer], moe_w_gate[layer],
                             moe_w_up[layer], moe_w_down[layer])
        h = _layernorm(DEEPNORM_ALPHA * h + (1.0 + gate2) * y, ln_w[layer, 1], ln_b[layer, 1])
    return h
```

```python
import functools

import jax
import jax.numpy as jnp
from jax import lax
from jax.experimental import pallas as pl
from jax.experimental.pallas import tpu as pltpu

F32 = jnp.float32
BF16 = jnp.bfloat16

HEAD_DIM = 128
CHUNK = 64
SUB = 16
CONV_WIDTH = 4
CONV_HALO = 8
NORM_EPS = 1e-6
MOE_TOP_K = 2
MOE_ROWS = 128
V7X_VMEM_BYTES = 64 * 1024 * 1024
VMEM_LIMIT = V7X_VMEM_BYTES - 8 * 1024 * 1024


def _params(*sem, vmem=VMEM_LIMIT):
    return pltpu.CompilerParams(dimension_semantics=sem, vmem_limit_bytes=vmem)


def _sigmoid(x):
    return 1.0 / (1.0 + jnp.exp(-x))


def _silu(x):
    return x * _sigmoid(x)


def _softplus(x):
    return jnp.maximum(x, 0.0) + jnp.log1p(jnp.exp(-jnp.abs(x)))


def _bdot(a, b):
    return jnp.dot(a.astype(BF16), b.astype(BF16), preferred_element_type=F32)


def _bdot_nt(a, b):
    return lax.dot_general(a.astype(BF16), b.astype(BF16), (((1,), (1,)), ((), ())),
                           preferred_element_type=F32)


def _bdot_tn(a, b):
    return lax.dot_general(a.astype(BF16), b.astype(BF16), (((0,), (0,)), ((), ())),
                           preferred_element_type=F32)


def _ada_kernel(c_ref, w_ref, b_ref, o_ref):
    cond = _silu(c_ref[...])
    o_ref[0] = jnp.sum(cond * w_ref[0], axis=0, keepdims=True) + b_ref[0]


def ada_modulation(c, ada_w, ada_b, tn=512):
    depth, d, n = ada_w.shape
    out = pl.pallas_call(
        _ada_kernel,
        grid=(depth, n // tn),
        in_specs=[pl.BlockSpec((d, 1), lambda l, j: (0, 0)),
                  pl.BlockSpec((1, d, tn), lambda l, j: (l, 0, j)),
                  pl.BlockSpec((1, 1, tn), lambda l, j: (l, 0, j))],
        out_specs=pl.BlockSpec((1, 1, tn), lambda l, j: (l, 0, j)),
        out_shape=jax.ShapeDtypeStruct((depth, 1, n), F32),
        compiler_params=_params("parallel", "parallel"),
        name="ada_modulation",
    )(c.reshape(d, 1), ada_w, ada_b.reshape(depth, 1, n))
    return out.reshape(depth * 6, d)


def _modulate_kernel(x_ref, mod_ref, u_ref, *, shift_row, scale_row):
    shift = mod_ref[shift_row:shift_row + 1, :]
    scale = mod_ref[scale_row:scale_row + 1, :]
    u_ref[...] = (x_ref[...] * (1.0 + scale) + shift).astype(u_ref.dtype)


def modulate(x, mod, shift_row, scale_row, tm=256):
    s, d = x.shape
    return pl.pallas_call(
        functools.partial(_modulate_kernel, shift_row=shift_row, scale_row=scale_row),
        grid=(s // tm,),
        in_specs=[pl.BlockSpec((tm, d), lambda i: (i, 0)),
                  pl.BlockSpec(mod.shape, lambda i: (0, 0))],
        out_specs=pl.BlockSpec((tm, d), lambda i: (i, 0)),
        out_shape=jax.ShapeDtypeStruct((s, d), BF16),
        compiler_params=_params("parallel"),
        name="modulate",
    )(x, mod)


def _mm_kernel(x_ref, w_ref, o_ref):
    o_ref[...] = jnp.dot(x_ref[...], w_ref[...], preferred_element_type=F32).astype(o_ref.dtype)


def matmul(x, w, tm=512, tn=1024, out_dtype=F32, name="matmul"):
    m, k = x.shape
    _, n = w.shape
    tm, tn = min(tm, m), min(tn, n)
    while n % tn:
        tn -= HEAD_DIM
    assert m % tm == 0 and tn > 0, (m, n, tm, tn)
    return pl.pallas_call(
        _mm_kernel,
        grid=(n // tn, m // tm),
        in_specs=[pl.BlockSpec((tm, k), lambda j, i: (i, 0)),
                  pl.BlockSpec((k, tn), lambda j, i: (0, j))],
        out_specs=pl.BlockSpec((tm, tn), lambda j, i: (i, j)),
        out_shape=jax.ShapeDtypeStruct((m, n), out_dtype),
        compiler_params=_params("parallel", "parallel"),
        name=name,
    )(x, w)


def _residual_norm(h, y, gate, lnw, lnb, alpha):
    v = alpha * h + (1.0 + gate) * y
    mu = jnp.mean(v, axis=-1, keepdims=True)
    dv = v - mu
    var = jnp.mean(dv * dv, axis=-1, keepdims=True)
    return dv * lax.rsqrt(var + NORM_EPS) * lnw + lnb


def _ln_kernel(h_ref, y_ref, mod_ref, lnw_ref, lnb_ref, *out_refs, gate_row, shift_row, scale_row, alpha):
    gate = mod_ref[gate_row:gate_row + 1, :]
    hn = _residual_norm(h_ref[...], y_ref[...], gate, lnw_ref[...], lnb_ref[...], alpha)
    out_refs[0][...] = hn
    if shift_row is not None:
        shift = mod_ref[shift_row:shift_row + 1, :]
        scale = mod_ref[scale_row:scale_row + 1, :]
        out_refs[1][...] = (hn * (1.0 + scale) + shift).astype(BF16)


def residual_norm(h, y, mod, lnw, lnb, gate_row, shift_row, scale_row, alpha, tm=256):
    s, d = h.shape
    row = pl.BlockSpec((tm, d), lambda i: (i, 0))
    vec = pl.BlockSpec((1, d), lambda i: (0, 0))
    with_u = shift_row is not None
    out_shape = [jax.ShapeDtypeStruct((s, d), F32)] + ([jax.ShapeDtypeStruct((s, d), BF16)] if with_u else [])
    outs = pl.pallas_call(
        functools.partial(_ln_kernel, gate_row=gate_row, shift_row=shift_row, scale_row=scale_row, alpha=alpha),
        grid=(s // tm,),
        in_specs=[row, row, pl.BlockSpec(mod.shape, lambda i: (0, 0)), vec, vec],
        out_specs=[row] * len(out_shape),
        out_shape=out_shape,
        compiler_params=_params("parallel"),
        name="residual_norm",
    )(h, y, mod, lnw.reshape(1, d), lnb.reshape(1, d))
    return (outs[0], outs[1]) if with_u else (outs[0], None)


def _gdn_gate_kernel(ab_ref, alog_ref, dtb_ref, gcum_ref, beta_ref):
    hv = alog_ref.shape[-1]
    a = ab_ref[:, :hv]
    b = ab_ref[:, hv:]
    g = -jnp.exp(alog_ref[...]) * _softplus(a + dtb_ref[...])
    beta_ref[...] = _sigmoid(b)
    tm = g.shape[0]
    row = lax.broadcasted_iota(jnp.int32, (CHUNK, CHUNK), 0)
    col = lax.broadcasted_iota(jnp.int32, (CHUNK, CHUNK), 1)
    tri = (row >= col).astype(F32)
    for c in range(tm // CHUNK):
        gcum_ref[c * CHUNK:(c + 1) * CHUNK, :] = jnp.dot(
            tri, g[c * CHUNK:(c + 1) * CHUNK, :], preferred_element_type=F32, precision=lax.Precision.HIGHEST)


def gdn_gates(ab, a_log, dt_bias, tm=512):
    s, two_hv = ab.shape
    hv = two_hv // 2
    tm = min(tm, s)
    vec = pl.BlockSpec((1, hv), lambda i: (0, 0))
    out = pl.BlockSpec((tm, hv), lambda i: (i, 0))
    return pl.pallas_call(
        _gdn_gate_kernel,
        grid=(s // tm,),
        in_specs=[pl.BlockSpec((tm, two_hv), lambda i: (i, 0)), vec, vec],
        out_specs=[out, out],
        out_shape=[jax.ShapeDtypeStruct((s, hv), F32)] * 2,
        compiler_params=_params("parallel"),
        name="gdn_gates",
    )(ab, a_log.reshape(1, hv), dt_bias.reshape(1, hv))


def _causal_conv_silu(buf_ref, raw_ref, w_ref, first):
    ts = raw_ref.shape[0]

    @pl.when(first)
    def _():
        buf_ref[0:CONV_HALO, :] = jnp.zeros((CONV_HALO, buf_ref.shape[1]), F32)

    buf_ref[CONV_HALO:CONV_HALO + ts, :] = raw_ref[...]
    acc = None
    for j in range(CONV_WIDTH):
        start = CONV_HALO - (CONV_WIDTH - 1) + j
        term = w_ref[j:j + 1, :] * buf_ref[start:start + ts, :]
        acc = term if acc is None else acc + term
    buf_ref[0:CONV_HALO, :] = buf_ref[ts:ts + CONV_HALO, :]
    return _silu(acc)


def _l2norm(x):
    return x * lax.rsqrt(jnp.sum(x * x, axis=-1, keepdims=True) + NORM_EPS)


def _unit_lower_inverse_minus_identity(a):
    row = lax.broadcasted_iota(jnp.int32, (CHUNK, CHUNK), 0)
    col = lax.broadcasted_iota(jnp.int32, (CHUNK, CHUNK), 1)
    same_sub = (row // SUB) == (col // SUB)
    b1 = jnp.where(same_sub, -a, 0.0)
    b2 = _bdot(b1, b1)
    e = b1 + b2 + _bdot(b1, b2)
    width = 4
    bp = b2
    while width < SUB:
        bp = _bdot(bp, bp)
        e = e + bp + _bdot(e, bp)
        width *= 2
    size = SUB
    while size < CHUNK:
        lower_pair = ((row // (2 * size)) == (col // (2 * size))) & ((row // size) > (col // size))
        f = jnp.where(lower_pair, a, 0.0)
        g = f + _bdot(e, f)
        e = e - (g + _bdot(g, e))
        size *= 2
    return e


def _gdn_kernel(q_ref, k_ref, v_ref, z_ref, wq_ref, wk_ref, wv_ref, gcol_ref, bcol_ref, grow_ref, nw_ref,
                o_ref, state_ref, qbuf, kbuf, vbuf):
    ts = q_ref.shape[0]
    first = pl.program_id(1) == 0

    @pl.when(first)
    def _():
        state_ref[...] = jnp.zeros(state_ref.shape, F32)

    q = _l2norm(_causal_conv_silu(qbuf, q_ref, wq_ref, first)) * (HEAD_DIM ** -0.5)
    k = _l2norm(_causal_conv_silu(kbuf, k_ref, wk_ref, first))
    v = _causal_conv_silu(vbuf, v_ref, wv_ref, first)
    rep = v.shape[1] // HEAD_DIM

    row = lax.broadcasted_iota(jnp.int32, (CHUNK, CHUNK), 0)
    col = lax.broadcasted_iota(jnp.int32, (CHUNK, CHUNK), 1)
    causal = row >= col
    strict = row > col
    nw = nw_ref[...]

    for c in range(ts // CHUNK):
        rows = slice(c * CHUNK, (c + 1) * CHUNK)
        qc, kc = q[rows], k[rows]
        kk = _bdot_nt(kc, kc)
        qk_raw = _bdot_nt(qc, kc)
        for r in range(rep):
            lanes = slice(r * HEAD_DIM, (r + 1) * HEAD_DIM)
            gcol = gcol_ref[0, rows, r:r + 1]
            bcol = bcol_ref[0, rows, r:r + 1]
            grow = grow_ref[0, r:r + 1, rows]
            glast = gcol[CHUNK - 1:CHUNK, :]
            decay = jnp.exp(jnp.where(causal, gcol - grow, -jnp.inf))
            a = jnp.where(strict, kk * bcol * decay, 0.0)
            e = _unit_lower_inverse_minus_identity(a)
            eg = jnp.exp(gcol)
            kb = kc * bcol
            rhs = jnp.concatenate([v[rows, lanes] * bcol, kb * eg], axis=1)
            sol = rhs + _bdot(e, rhs)
            u_c, w_c = sol[:, :HEAD_DIM], sol[:, HEAD_DIM:]
            state = state_ref[r]
            ws_qs = _bdot(jnp.concatenate([w_c, qc * eg], axis=0), state)
            v_new = u_c - ws_qs[:CHUNK]
            qk = jnp.where(causal, qk_raw * decay, 0.0)
            o = ws_qs[CHUNK:] + _bdot(qk, v_new)
            k_dec = kc * jnp.exp(glast - gcol)
            state_ref[r] = state * jnp.exp(glast) + _bdot_tn(k_dec, v_new)
            o = o * lax.rsqrt(jnp.mean(o * o, axis=-1, keepdims=True) + NORM_EPS) * nw
            o_ref[rows, lanes] = (o * _silu(z_ref[rows, lanes])).astype(o_ref.dtype)


def gdn_recurrence(proj, conv_w, gcum, beta, norm_w, key_dim, value_dim, ts=256):
    s = proj.shape[0]
    hq = key_dim // HEAD_DIM
    hv = value_dim // HEAD_DIM
    rep = hv // hq
    vw = rep * HEAD_DIM
    ts = min(ts, s)
    nchunk = s // CHUNK
    gcol = gcum.reshape(s, hq, rep).transpose(1, 0, 2)
    bcol = beta.reshape(s, hq, rep).transpose(1, 0, 2)
    grow = gcum.T.reshape(hq, rep, s)
    k_off = key_dim // HEAD_DIM
    v_off = 2 * key_dim // vw
    z_off = (2 * key_dim + value_dim) // vw
    return pl.pallas_call(
        _gdn_kernel,
        grid=(hq, s // ts),
        in_specs=[pl.BlockSpec((ts, HEAD_DIM), lambda h, i: (i, h)),
                  pl.BlockSpec((ts, HEAD_DIM), lambda h, i: (i, k_off + h)),
                  pl.BlockSpec((ts, vw), lambda h, i: (i, v_off + h)),
                  pl.BlockSpec((ts, vw), lambda h, i: (i, z_off + h)),
                  pl.BlockSpec((CONV_WIDTH, HEAD_DIM), lambda h, i: (0, h)),
                  pl.BlockSpec((CONV_WIDTH, HEAD_DIM), lambda h, i: (0, k_off + h)),
                  pl.BlockSpec((CONV_WIDTH, vw), lambda h, i: (0, v_off + h)),
                  pl.BlockSpec((1, ts, rep), lambda h, i: (h, i, 0)),
                  pl.BlockSpec((1, ts, rep), lambda h, i: (h, i, 0)),
                  pl.BlockSpec((1, rep, ts), lambda h, i: (h, 0, i)),
                  pl.BlockSpec((1, HEAD_DIM), lambda h, i: (0, 0))],
        out_specs=pl.BlockSpec((ts, vw), lambda h, i: (i, h)),
        out_shape=jax.ShapeDtypeStruct((s, value_dim), BF16),
        scratch_shapes=[pltpu.VMEM((rep, HEAD_DIM, HEAD_DIM), F32),
                        pltpu.VMEM((ts + CONV_HALO, HEAD_DIM), F32),
                        pltpu.VMEM((ts + CONV_HALO, HEAD_DIM), F32),
                        pltpu.VMEM((ts + CONV_HALO, vw), F32)],
        compiler_params=_params("parallel", "arbitrary"),
        name="gdn_recurrence",
    )(proj, proj, proj, proj, conv_w, conv_w, conv_w, gcol, bcol, grow, norm_w.reshape(1, HEAD_DIM))


def gated_deltanet_mixer(u, w_in, conv_w, a_log, dt_bias, norm_w, w_out):
    hv = a_log.shape[0]
    value_dim = hv * HEAD_DIM
    key_dim = (w_in.shape[1] - 2 * value_dim - 2 * hv) // 2
    n_main = 2 * key_dim + 2 * value_dim
    proj = matmul(u, w_in[:, :n_main].astype(BF16), name="gdn_in_proj")
    ab = matmul(u, w_in[:, n_main:].astype(BF16), name="gdn_gate_proj")
    gcum, beta = gdn_gates(ab, a_log, dt_bias)
    o = gdn_recurrence(proj, conv_w, gcum, beta, norm_w, key_dim, value_dim)
    return matmul(o, w_out.astype(BF16), tn=512, name="gdn_out_proj")


def _block_rows(x, sub, offset):
    n = x.shape[0] // sub
    x3 = x.reshape(n, sub, x.shape[1])
    return jnp.broadcast_to(x3[:, offset:offset + 1, :], x3.shape).reshape(x.shape)


def _gla_intra_chunk(q, k, b):
    row = lax.broadcasted_iota(jnp.int32, (CHUNK, CHUNK), 0)
    col = lax.broadcasted_iota(jnp.int32, (CHUNK, CHUNK), 1)
    att = jnp.zeros((CHUNK, CHUNK), F32)
    size = CHUNK // 2
    while size >= SUB:
        b_row_ref = _block_rows(b, size, 0)
        nxt = jnp.concatenate([b_row_ref[size:], jnp.broadcast_to(b[CHUNK - 1:CHUNK], (size, b.shape[1]))], axis=0)
        qs = q * jnp.exp(b - b_row_ref)
        ks = k * jnp.exp(nxt - b)
        pair = ((row // size) % 2 == 1) & ((row // size) == (col // size) + 1)
        att = att + jnp.where(pair, _bdot_nt(qs, ks), 0.0)
        size //= 2
    row_in = lax.broadcasted_iota(jnp.int32, (CHUNK, 1), 0) % SUB
    same_sub = (row // SUB) == (col // SUB)
    for j in range(SUB):
        bj = _block_rows(b, SUB, j)
        kj = _block_rows(k, SUB, j)
        e = jnp.exp(jnp.where(row_in >= j, b - bj, -jnp.inf))
        s = jnp.sum(q * e * kj, axis=-1, keepdims=True)
        att = jnp.where(same_sub & ((col % SUB) == j), s, att)
    return att


def _hgrn_lower_bound(lbl_ref, layer):
    depth = lbl_ref.shape[0]
    rows = [lbl_ref[i:i + 1, :] for i in range(depth)]
    m = functools.reduce(jnp.maximum, rows)
    ex = [jnp.exp(r - m) for r in rows]
    total = functools.reduce(lambda x, y: x + y, ex)
    acc = jnp.zeros_like(m)
    for i in range(1, layer + 1):
        acc = acc + ex[i] / total
    return acc


def _hgrn_kernel(q_ref, f_ref, i_ref, og_ref, lbl_ref, nw_ref, o_ref, state_ref, *, layer):
    ts = q_ref.shape[0]

    @pl.when(pl.program_id(1) == 0)
    def _():
        state_ref[...] = jnp.zeros(state_ref.shape, F32)

    lb = _hgrn_lower_bound(lbl_ref, layer)
    f = f_ref[...]
    log_sig = -_softplus(-f)
    t0 = jnp.log(lb)
    t1 = jnp.log1p(-lb) + log_sig
    log_f = jnp.maximum(t0, t1) + jnp.log1p(jnp.exp(-jnp.abs(t0 - t1)))
    q = _silu(q_ref[...])
    k = (1.0 - lb) * _sigmoid(-f)
    row = lax.broadcasted_iota(jnp.int32, (CHUNK, CHUNK), 0)
    col = lax.broadcasted_iota(jnp.int32, (CHUNK, CHUNK), 1)
    tri = (row >= col).astype(F32)
    nw = nw_ref[...]
    for c in range(ts // CHUNK):
        rows = slice(c * CHUNK, (c + 1) * CHUNK)
        qc, kc, vc = q[rows], k[rows], i_ref[rows, :]
        b = jnp.dot(tri, log_f[rows], preferred_element_type=F32, precision=lax.Precision.HIGHEST)
        b_last = b[CHUNK - 1:CHUNK, :]
        att = _gla_intra_chunk(qc, kc, b)
        state_t = state_ref[...]
        o = _bdot(att, vc) + _bdot_nt(qc * jnp.exp(b), state_t)
        state_ref[...] = state_t * jnp.exp(b_last) + _bdot_tn(vc, kc * jnp.exp(b_last - b))
        o = o * lax.rsqrt(jnp.mean(o * o, axis=-1, keepdims=True) + NORM_EPS) * nw
        o_ref[rows, :] = (o * _silu(og_ref[rows, :])).astype(o_ref.dtype)


def hgrn_recurrence(proj, lb_logits, layer, norm_w, ts=256):
    s, n = proj.shape
    d = n // 4
    heads = d // HEAD_DIM
    ts = min(ts, s)
    depth = lb_logits.shape[0]

    def col(off):
        return pl.BlockSpec((ts, HEAD_DIM), lambda h, i: (i, off * heads + h))

    return pl.pallas_call(
        functools.partial(_hgrn_kernel, layer=layer),
        grid=(heads, s // ts),
        in_specs=[col(0), col(1), col(2), col(3),
                  pl.BlockSpec((depth, HEAD_DIM), lambda h, i: (0, h)),
                  pl.BlockSpec((1, HEAD_DIM), lambda h, i: (0, 0))],
        out_specs=pl.BlockSpec((ts, HEAD_DIM), lambda h, i: (i, h)),
        out_shape=jax.ShapeDtypeStruct((s, d), BF16),
        scratch_shapes=[pltpu.VMEM((HEAD_DIM, HEAD_DIM), F32)],
        compiler_params=_params("parallel", "arbitrary"),
        name="hgrn_recurrence",
    )(proj, proj, proj, proj, lb_logits, norm_w.reshape(1, HEAD_DIM))


def hgrn2_mixer(u, w_in, lb_logits, layer, norm_w, w_out):
    proj = matmul(u, w_in.astype(BF16), name="hgrn_in_proj")
    o = hgrn_recurrence(proj, lb_logits, layer, norm_w)
    return matmul(o, w_out.astype(BF16), name="hgrn_out_proj")


ROUTER_LANES = 128
LANE_SENTINEL = 1 << 20


def _router_kernel(h_ref, mod_ref, w_ref, eid_ref, gate_ref, *, shift_row, scale_row, groups, epg_shift):
    shift = mod_ref[shift_row:shift_row + 1, :]
    scale = mod_ref[scale_row:scale_row + 1, :]
    u = h_ref[...] * (1.0 + scale) + shift
    logits = jnp.dot(u, w_ref[...], preferred_element_type=F32, precision=lax.Precision.HIGHEST)
    lane = lax.broadcasted_iota(jnp.int32, logits.shape, 1)
    n_exp = groups << epg_shift

    def first_argmax(vals):
        m = jnp.max(vals, axis=-1, keepdims=True)
        idx = jnp.min(jnp.where(vals == m, lane, LANE_SENTINEL), axis=-1, keepdims=True)
        return m, idx

    is_group = lane < groups
    gmax, grp = first_argmax(jnp.where(is_group, logits, -jnp.inf))
    p_grp = 1.0 / jnp.sum(jnp.where(is_group, jnp.exp(logits - gmax), 0.0), axis=-1, keepdims=True)
    e_lane = lane - groups
    in_grp = (e_lane >= 0) & (e_lane < n_exp) & (lax.shift_right_arithmetic(e_lane, epg_shift) == grp)
    el = jnp.where(in_grp, logits, -jnp.inf)
    m1, i1 = first_argmax(el)
    z = jnp.sum(jnp.where(in_grp, jnp.exp(logits - m1), 0.0), axis=-1, keepdims=True)
    m2, i2 = first_argmax(jnp.where(lane == i1, -jnp.inf, el))
    p1 = 1.0 / z
    p2 = jnp.exp(m2 - m1) / z
    denom = p1 + p2
    eid_ref[...] = jnp.where(lane == 0, i1 - groups, jnp.where(lane == 1, i2 - groups, 0))
    gate_ref[...] = jnp.where(lane == 0, p_grp * p1 / denom, jnp.where(lane == 1, p_grp * p2 / denom, 0.0))


def moe_router(h, mod, shift_row, scale_row, w_group, w_expert, tm=256):
    s, d = h.shape
    groups = w_group.shape[1]
    n_exp = w_expert.shape[1]
    epg = n_exp // groups
    assert epg & (epg - 1) == 0 and groups + n_exp <= ROUTER_LANES
    w = jnp.concatenate([w_group, w_expert, jnp.zeros((d, ROUTER_LANES - groups - n_exp), F32)], axis=1)
    out = pl.BlockSpec((tm, ROUTER_LANES), lambda i: (i, 0))
    eid, gate = pl.pallas_call(
        functools.partial(_router_kernel, shift_row=shift_row, scale_row=scale_row, groups=groups,
                          epg_shift=epg.bit_length() - 1),
        grid=(s // tm,),
        in_specs=[pl.BlockSpec((tm, d), lambda i: (i, 0)),
                  pl.BlockSpec(mod.shape, lambda i: (0, 0)),
                  pl.BlockSpec((d, ROUTER_LANES), lambda i: (0, 0))],
        out_specs=[out, out],
        out_shape=[jax.ShapeDtypeStruct((s, ROUTER_LANES), jnp.int32), jax.ShapeDtypeStruct((s, ROUTER_LANES), F32)],
        compiler_params=_params("parallel"),
        name="moe_router",
    )(h, mod, w)
    return eid[:, :MOE_TOP_K], gate[:, :MOE_TOP_K]


def _dispatch_plan(eid, n_exp):
    s = eid.shape[0]
    a = s * MOE_TOP_K
    e_flat = eid.reshape(a)
    onehot = (e_flat[:, None] == jnp.arange(n_exp, dtype=jnp.int32)[None, :]).astype(jnp.int32)
    csum = jnp.cumsum(onehot, axis=0)
    rank = jnp.sum(onehot * csum, axis=1) - 1
    counts = csum[-1]
    padded = (counts + MOE_ROWS - 1) // MOE_ROWS * MOE_ROWS
    pad_end = jnp.cumsum(padded)
    pad_start = pad_end - padded
    dest = pad_start[e_flat] + rank
    n_blk = -(-a // MOE_ROWS) + n_exp
    tok_buf = jnp.zeros((n_blk * MOE_ROWS,), jnp.int32).at[dest].set(jnp.arange(a, dtype=jnp.int32) // MOE_TOP_K)
    blk_start = jnp.arange(n_blk, dtype=jnp.int32) * MOE_ROWS
    blk_e = jnp.minimum(jnp.searchsorted(pad_end, blk_start, side="right"), n_exp - 1).astype(jnp.int32)
    blk_used = (blk_start < pad_end[-1]).astype(jnp.int32)
    return tok_buf, blk_e, blk_used, dest.reshape(s, MOE_TOP_K)


def _gather_rows_kernel(tok_ref, src_hbm, dst_hbm, sem):
    rows = tok_ref.shape[-1]
    base = pl.program_id(0) * rows

    def row_copy(r):
        return pltpu.make_async_copy(src_hbm.at[tok_ref[0, 0, r]], dst_hbm.at[base + r], sem)

    def start(r, carry):
        row_copy(r).start()
        return carry

    def wait(r, carry):
        row_copy(r).wait()
        return carry

    lax.fori_loop(0, rows, start, 0)
    lax.fori_loop(0, rows, wait, 0)


def gather_rows(x, idx, rows_per_step=1024):
    n, d = x.shape
    p = idx.shape[0]
    rows_per_step = min(rows_per_step, p)
    while p % rows_per_step:
        rows_per_step -= MOE_ROWS
    steps = p // rows_per_step
    out = pl.pallas_call(
        _gather_rows_kernel,
        grid=(steps,),
        in_specs=[pl.BlockSpec((1, 1, rows_per_step), lambda b: (b, 0, 0), memory_space=pltpu.SMEM),
                  pl.BlockSpec(memory_space=pl.ANY)],
        out_specs=pl.BlockSpec(memory_space=pl.ANY),
        out_shape=jax.ShapeDtypeStruct((p, d // HEAD_DIM, HEAD_DIM), x.dtype),
        scratch_shapes=[pltpu.SemaphoreType.DMA(())],
        compiler_params=_params("arbitrary"),
        name="gather_rows",
    )(idx.reshape(steps, 1, rows_per_step), x.reshape(n, d // HEAD_DIM, HEAD_DIM))
    return out.reshape(p, d)


def _moe_ffn_kernel(blk_e_ref, blk_used_ref, x_ref, wg_ref, wu_ref, wd_ref, y_ref, wg_bf, wu_bf, wd_bf):
    b = pl.program_id(0)
    prev = blk_e_ref[jnp.maximum(b - 1, 0)]
    used = blk_used_ref[b] > 0

    @pl.when(used & ((b == 0) | (blk_e_ref[b] != prev)))
    def _():
        wg_bf[...] = wg_ref[0, 0].astype(BF16)
        wu_bf[...] = wu_ref[0, 0].astype(BF16)
        wd_bf[...] = wd_ref[0, 0].astype(BF16)

    @pl.when(used)
    def _():
        x = x_ref[...]
        g = jnp.dot(x, wg_bf[...], preferred_element_type=F32)
        up = jnp.dot(x, wu_bf[...], preferred_element_type=F32)
        hid = (_silu(g) * up).astype(BF16)
        y_ref[...] = jnp.dot(hid, wd_bf[...], preferred_element_type=F32)

    @pl.when(jnp.logical_not(used))
    def _():
        y_ref[...] = jnp.zeros(y_ref.shape, F32)


def moe_ffn(xs, blk_e, blk_used, w_gate, w_up, w_down, layer):
    p, d = xs.shape
    f = w_gate.shape[-1]
    n_blk = p // MOE_ROWS
    grid_spec = pltpu.PrefetchScalarGridSpec(
        num_scalar_prefetch=2,
        grid=(n_blk,),
        in_specs=[pl.BlockSpec((MOE_ROWS, d), lambda b, e, u: (b, 0)),
                  pl.BlockSpec((1, 1, d, f), lambda b, e, u: (layer, e[b], 0, 0)),
                  pl.BlockSpec((1, 1, d, f), lambda b, e, u: (layer, e[b], 0, 0)),
                  pl.BlockSpec((1, 1, f, d), lambda b, e, u: (layer, e[b], 0, 0))],
        out_specs=pl.BlockSpec((MOE_ROWS, d), lambda b, e, u: (b, 0)),
        scratch_shapes=[pltpu.VMEM((d, f), BF16), pltpu.VMEM((d, f), BF16), pltpu.VMEM((f, d), BF16)],
    )
    return pl.pallas_call(
        _moe_ffn_kernel,
        grid_spec=grid_spec,
        out_shape=jax.ShapeDtypeStruct((p, d), F32),
        compiler_params=_params("arbitrary"),
        name="moe_ffn",
    )(blk_e, blk_used, xs, w_gate, w_up, w_down)


def _combine_kernel(h_ref, yk_ref, gw_ref, mod_ref, lnw_ref, lnb_ref, *out_refs,
                    gate_row, shift_row, scale_row, alpha):
    y = gw_ref[:, 0:1] * yk_ref[0]
    for k in range(1, MOE_TOP_K):
        y = y + gw_ref[:, k:k + 1] * yk_ref[k]
    gate = mod_ref[gate_row:gate_row + 1, :]
    hn = _residual_norm(h_ref[...], y, gate, lnw_ref[...], lnb_ref[...], alpha)
    out_refs[0][...] = hn
    if shift_row is not None:
        shift = mod_ref[shift_row:shift_row + 1, :]
        scale = mod_ref[scale_row:scale_row + 1, :]
        out_refs[1][...] = (hn * (1.0 + scale) + shift).astype(BF16)


def moe_combine_norm(h, yk, gate_w, mod, lnw, lnb, gate_row, shift_row, scale_row, alpha, tm=128):
    s, d = h.shape
    tm = min(tm, s)
    row = pl.BlockSpec((tm, d), lambda i: (i, 0))
    vec = pl.BlockSpec((1, d), lambda i: (0, 0))
    with_u = shift_row is not None
    out_shape = [jax.ShapeDtypeStruct((s, d), F32)] + ([jax.ShapeDtypeStruct((s, d), BF16)] if with_u else [])
    outs = pl.pallas_call(
        functools.partial(_combine_kernel, gate_row=gate_row, shift_row=shift_row, scale_row=scale_row, alpha=alpha),
        grid=(s // tm,),
        in_specs=[row,
                  pl.BlockSpec((MOE_TOP_K, tm, d), lambda i: (0, i, 0)),
                  pl.BlockSpec((tm, MOE_TOP_K), lambda i: (i, 0)),
                  pl.BlockSpec(mod.shape, lambda i: (0, 0)), vec, vec],
        out_specs=[row] * len(out_shape),
        out_shape=out_shape,
        compiler_params=_params("parallel"),
        name="moe_combine_norm",
    )(h, yk, gate_w, mod, lnw.reshape(1, d), lnb.reshape(1, d))
    return (outs[0], outs[1]) if with_u else (outs[0], None)


def moe_sublayer(h, u, mod, lnw, lnb, w_group, w_expert, w_gate, w_up, w_down, layer,
                 cur_shift_row, cur_scale_row, gate_row, shift_row, scale_row, alpha):
    s, d = h.shape
    eid, gate_w = moe_router(h, mod, cur_shift_row, cur_scale_row, w_group, w_expert)
    tok_buf, blk_e, blk_used, pos = _dispatch_plan(eid, w_expert.shape[1])
    xs = gather_rows(u, tok_buf)
    ys = moe_ffn(xs, blk_e, blk_used, w_gate, w_up, w_down, layer)
    yk = gather_rows(ys, pos.T.reshape(MOE_TOP_K * s)).reshape(MOE_TOP_K, s, d)
    return moe_combine_norm(h, yk, gate_w, mod, lnw, lnb, gate_row, shift_row, scale_row, alpha)


def kernel(x, c, ada_w, ada_b, ln_w, ln_b, gdn_w_in, gdn_conv_w, gdn_a_log, gdn_dt_bias, gdn_norm_w, gdn_w_out,
           hgrn_w_in, hgrn_lb_logits, hgrn_norm_w, hgrn_w_out, moe_w_group, moe_w_expert, moe_w_gate, moe_w_up,
           moe_w_down):
    batch, s, d = x.shape
    assert batch == 1, "one sequence per call"
    depth = ada_w.shape[0]
    alpha = (2 * depth) ** 0.25
    mod = ada_modulation(c, ada_w, ada_b)
    h = x.reshape(s, d)
    u = modulate(h, mod, 0, 1)
    for layer in range(depth):
        base = 6 * layer
        j = layer // 2
        if layer % 2 == 0:
            y = gated_deltanet_mixer(u, gdn_w_in[j], gdn_conv_w[j], gdn_a_log[j], gdn_dt_bias[j], gdn_norm_w[j],
                                     gdn_w_out[j])
        else:
            y = hgrn2_mixer(u, hgrn_w_in[j], hgrn_lb_logits, layer, hgrn_norm_w[j], hgrn_w_out[j])
        h, u = residual_norm(h, y, mod, ln_w[layer, 0], ln_b[layer, 0], base + 2, base + 3, base + 4, alpha)
        last = layer + 1 == depth
        h, u = moe_sublayer(h, u, mod, ln_w[layer, 1], ln_b[layer, 1], moe_w_group[layer], moe_w_expert[layer],
                            moe_w_gate, moe_w_up, moe_w_down, layer,
                            cur_shift_row=base + 3, cur_scale_row=base + 4, gate_row=base + 5,
                            shift_row=None if last else base + 6, scale_row=None if last else base + 7, alpha=alpha)
    return h.reshape(batch, s, d)
```

```python
import functools

import jax
import jax.numpy as jnp
from jax import lax
from jax.experimental import pallas as pl
from jax.experimental.pallas import tpu as pltpu

F32 = jnp.float32
BF16 = jnp.bfloat16

HEAD_DIM = 128
CHUNK = 64
SUB = 16
CONV_WIDTH = 4
CONV_HALO = 8
NORM_EPS = 1e-6
MOE_TOP_K = 2
MOE_ROWS = 128
V7X_VMEM_BYTES = 64 * 1024 * 1024
VMEM_LIMIT = V7X_VMEM_BYTES - 8 * 1024 * 1024


def _params(*sem, vmem=VMEM_LIMIT):
    return pltpu.CompilerParams(dimension_semantics=sem, vmem_limit_bytes=vmem)


def _sigmoid(x):
    return 1.0 / (1.0 + jnp.exp(-x))


def _silu(x):
    return x * _sigmoid(x)


def _softplus(x):
    return jnp.maximum(x, 0.0) + jnp.log1p(jnp.exp(-jnp.abs(x)))


def _bdot(a, b):
    return jnp.dot(a.astype(BF16), b.astype(BF16), preferred_element_type=F32)


def _bdot_nt(a, b):
    return lax.dot_general(a.astype(BF16), b.astype(BF16), (((1,), (1,)), ((), ())),
                           preferred_element_type=F32)


def _bdot_tn(a, b):
    return lax.dot_general(a.astype(BF16), b.astype(BF16), (((0,), (0,)), ((), ())),
                           preferred_element_type=F32)


def _ada_kernel(c_ref, w_ref, b_ref, o_ref):
    cond = _silu(c_ref[...])
    o_ref[0] = jnp.sum(cond * w_ref[0], axis=0, keepdims=True) + b_ref[0]


def ada_modulation(c, ada_w, ada_b, tn=512):
    depth, d, n = ada_w.shape
    out = pl.pallas_call(
        _ada_kernel,
        grid=(depth, n // tn),
        in_specs=[pl.BlockSpec((d, 1), lambda l, j: (0, 0)),
                  pl.BlockSpec((1, d, tn), lambda l, j: (l, 0, j)),
                  pl.BlockSpec((1, 1, tn), lambda l, j: (l, 0, j))],
        out_specs=pl.BlockSpec((1, 1, tn), lambda l, j: (l, 0, j)),
        out_shape=jax.ShapeDtypeStruct((depth, 1, n), F32),
        compiler_params=_params("parallel", "parallel"),
        name="ada_modulation",
    )(c.reshape(d, 1), ada_w, ada_b.reshape(depth, 1, n))
    return out.reshape(depth * 6, d)


def _modulate_kernel(x_ref, mod_ref, u_ref, *, shift_row, scale_row):
    shift = mod_ref[shift_row:shift_row + 1, :]
    scale = mod_ref[scale_row:scale_row + 1, :]
    u_ref[...] = (x_ref[...] * (1.0 + scale) + shift).astype(u_ref.dtype)


def modulate(x, mod, shift_row, scale_row, tm=256):
    s, d = x.shape
    return pl.pallas_call(
        functools.partial(_modulate_kernel, shift_row=shift_row, scale_row=scale_row),
        grid=(s // tm,),
        in_specs=[pl.BlockSpec((tm, d), lambda i: (i, 0)),
                  pl.BlockSpec(mod.shape, lambda i: (0, 0))],
        out_specs=pl.BlockSpec((tm, d), lambda i: (i, 0)),
        out_shape=jax.ShapeDtypeStruct((s, d), BF16),
        compiler_params=_params("parallel"),
        name="modulate",
    )(x, mod)


def _mm_kernel(x_ref, w_ref, o_ref):
    o_ref[...] = jnp.dot(x_ref[...], w_ref[...], preferred_element_type=F32).astype(o_ref.dtype)


def matmul(x, w, tm=512, tn=1024, out_dtype=F32, name="matmul"):
    m, k = x.shape
    _, n = w.shape
    tm, tn = min(tm, m), min(tn, n)
    while n % tn:
        tn -= HEAD_DIM
    assert m % tm == 0 and tn > 0, (m, n, tm, tn)
    return pl.pallas_call(
        _mm_kernel,
        grid=(n // tn, m // tm),
        in_specs=[pl.BlockSpec((tm, k), lambda j, i: (i, 0)),
                  pl.BlockSpec((k, tn), lambda j, i: (0, j))],
        out_specs=pl.BlockSpec((tm, tn), lambda j, i: (i, j)),
        out_shape=jax.ShapeDtypeStruct((m, n), out_dtype),
        compiler_params=_params("parallel", "parallel"),
        name=name,
    )(x, w)


def _residual_norm(h, y, gate, lnw, lnb, alpha):
    v = alpha * h + (1.0 + gate) * y
    mu = jnp.mean(v, axis=-1, keepdims=True)
    dv = v - mu
    var = jnp.mean(dv * dv, axis=-1, keepdims=True)
    return dv * lax.rsqrt(var + NORM_EPS) * lnw + lnb


HI_HALF_MASK = 0xFFFF0000


def _pack_bf16_halves(u):
    half = u.shape[1] // 2
    lo = pltpu.bitcast(u[:, :half].astype(BF16).astype(F32), jnp.uint32)
    hi = pltpu.bitcast(u[:, half:].astype(BF16).astype(F32), jnp.uint32)
    return (lo >> 16) | (hi & jnp.uint32(HI_HALF_MASK))


def _unpack_bf16_halves(words):
    lo = pltpu.bitcast(words << 16, F32).astype(BF16)
    hi = pltpu.bitcast(words & jnp.uint32(HI_HALF_MASK), F32).astype(BF16)
    return lo, hi


def _ln_kernel(h_ref, y_ref, mod_ref, lnw_ref, lnb_ref, h_out, up_out, *, gate_row, shift_row, scale_row, alpha):
    gate = mod_ref[gate_row:gate_row + 1, :]
    hn = _residual_norm(h_ref[...], y_ref[...], gate, lnw_ref[...], lnb_ref[...], alpha)
    h_out[...] = hn
    shift = mod_ref[shift_row:shift_row + 1, :]
    scale = mod_ref[scale_row:scale_row + 1, :]
    words = _pack_bf16_halves(hn * (1.0 + scale) + shift)
    for j in range(up_out.shape[1]):
        up_out[:, j, :] = words[:, j * HEAD_DIM:(j + 1) * HEAD_DIM]


def residual_norm(h, y, mod, lnw, lnb, gate_row, shift_row, scale_row, alpha, tm=256):
    s, d = h.shape
    row = pl.BlockSpec((tm, d), lambda i: (i, 0))
    vec = pl.BlockSpec((1, d), lambda i: (0, 0))
    slabs = d // (2 * HEAD_DIM)
    return pl.pallas_call(
        functools.partial(_ln_kernel, gate_row=gate_row, shift_row=shift_row, scale_row=scale_row, alpha=alpha),
        grid=(s // tm,),
        in_specs=[row, row, pl.BlockSpec(mod.shape, lambda i: (0, 0)), vec, vec],
        out_specs=[row, pl.BlockSpec((tm, slabs, HEAD_DIM), lambda i: (i, 0, 0))],
        out_shape=[jax.ShapeDtypeStruct((s, d), F32), jax.ShapeDtypeStruct((s, slabs, HEAD_DIM), jnp.uint32)],
        compiler_params=_params("parallel"),
        name="residual_norm",
    )(h, y, mod, lnw.reshape(1, d), lnb.reshape(1, d))


def _gdn_gate_kernel(ab_ref, alog_ref, dtb_ref, gcum_ref, beta_ref):
    hv = alog_ref.shape[-1]
    a = ab_ref[:, :hv]
    b = ab_ref[:, hv:]
    g = -jnp.exp(alog_ref[...]) * _softplus(a + dtb_ref[...])
    beta_ref[...] = _sigmoid(b)
    tm = g.shape[0]
    row = lax.broadcasted_iota(jnp.int32, (CHUNK, CHUNK), 0)
    col = lax.broadcasted_iota(jnp.int32, (CHUNK, CHUNK), 1)
    tri = (row >= col).astype(F32)
    for c in range(tm // CHUNK):
        gcum_ref[c * CHUNK:(c + 1) * CHUNK, :] = jnp.dot(
            tri, g[c * CHUNK:(c + 1) * CHUNK, :], preferred_element_type=F32, precision=lax.Precision.HIGHEST)


def gdn_gates(ab, a_log, dt_bias, tm=512):
    s, two_hv = ab.shape
    hv = two_hv // 2
    tm = min(tm, s)
    vec = pl.BlockSpec((1, hv), lambda i: (0, 0))
    out = pl.BlockSpec((tm, hv), lambda i: (i, 0))
    return pl.pallas_call(
        _gdn_gate_kernel,
        grid=(s // tm,),
        in_specs=[pl.BlockSpec((tm, two_hv), lambda i: (i, 0)), vec, vec],
        out_specs=[out, out],
        out_shape=[jax.ShapeDtypeStruct((s, hv), F32)] * 2,
        compiler_params=_params("parallel"),
        name="gdn_gates",
    )(ab, a_log.reshape(1, hv), dt_bias.reshape(1, hv))


def _causal_conv_silu(buf_ref, raw_ref, w_ref, first):
    ts = raw_ref.shape[0]

    @pl.when(first)
    def _():
        buf_ref[0:CONV_HALO, :] = jnp.zeros((CONV_HALO, buf_ref.shape[1]), F32)

    buf_ref[CONV_HALO:CONV_HALO + ts, :] = raw_ref[...]
    acc = None
    for j in range(CONV_WIDTH):
        start = CONV_HALO - (CONV_WIDTH - 1) + j
        term = w_ref[j:j + 1, :] * buf_ref[start:start + ts, :]
        acc = term if acc is None else acc + term
    buf_ref[0:CONV_HALO, :] = buf_ref[ts:ts + CONV_HALO, :]
    return _silu(acc)


def _l2norm(x):
    return x * lax.rsqrt(jnp.sum(x * x, axis=-1, keepdims=True) + NORM_EPS)


def _unit_lower_inverse_minus_identity(mats):
    n = range(len(mats))
    row = lax.broadcasted_iota(jnp.int32, mats[0].shape, 0)
    col = lax.broadcasted_iota(jnp.int32, mats[0].shape, 1)
    same_sub = (row // SUB) == (col // SUB)
    b1 = [jnp.where(same_sub, -a, 0.0) for a in mats]
    bp = [_bdot(b1[i], b1[i]) for i in n]
    e = [b1[i] + bp[i] + _bdot(b1[i], bp[i]) for i in n]
    width = 4
    while width < SUB:
        bp = [_bdot(bp[i], bp[i]) for i in n]
        e = [e[i] + bp[i] + _bdot(e[i], bp[i]) for i in n]
        width *= 2
    size = SUB
    while size < CHUNK:
        lower_pair = ((row // (2 * size)) == (col // (2 * size))) & ((row // size) > (col // size))
        f = [jnp.where(lower_pair, a, 0.0) for a in mats]
        g = [f[i] + _bdot(e[i], f[i]) for i in n]
        e = [e[i] - (g[i] + _bdot(g[i], e[i])) for i in n]
        size *= 2
    return e


def _gdn_kernel(q_ref, k_ref, v_ref, z_ref, wq_ref, wk_ref, wv_ref, gcol_ref, bcol_ref, grow_ref, nw_ref,
                o_ref, state_ref, qbuf, kbuf, vbuf):
    ts = q_ref.shape[0]
    first = pl.program_id(1) == 0

    @pl.when(first)
    def _():
        state_ref[...] = jnp.zeros(state_ref.shape, F32)

    q_all = _causal_conv_silu(qbuf, q_ref, wq_ref, first)
    k_all = _causal_conv_silu(kbuf, k_ref, wk_ref, first)
    v_all = _causal_conv_silu(vbuf, v_ref, wv_ref, first)
    nh = q_all.shape[1] // HEAD_DIM
    rep = v_all.shape[1] // q_all.shape[1]

    row = lax.broadcasted_iota(jnp.int32, (ts, ts), 0)
    col = lax.broadcasted_iota(jnp.int32, (ts, ts), 1)
    same_chunk = (row // CHUNK) == (col // CHUNK)
    causal = same_chunk & (row >= col)
    strict = same_chunk & (row > col)
    nw = nw_ref[...]

    heads = range(nh)
    chains = range(nh * rep)
    head_of = [c // rep for c in chains]
    lanes = [slice(c * HEAD_DIM, (c + 1) * HEAD_DIM) for c in chains]
    q = [_l2norm(q_all[:, h * HEAD_DIM:(h + 1) * HEAD_DIM]) * (HEAD_DIM ** -0.5) for h in heads]
    k = [_l2norm(k_all[:, h * HEAD_DIM:(h + 1) * HEAD_DIM]) for h in heads]
    kk = [_bdot_nt(k[h], k[h]) for h in heads]
    qk_raw = [_bdot_nt(q[h], k[h]) for h in heads]
    gcol = [gcol_ref[c // rep, :, c % rep:c % rep + 1] for c in chains]
    bcol = [bcol_ref[c // rep, :, c % rep:c % rep + 1] for c in chains]
    grow = [grow_ref[c // rep, c % rep:c % rep + 1, :] for c in chains]
    glast = [_block_rows(gcol[c], CHUNK, CHUNK - 1) for c in chains]
    decay = [jnp.exp(jnp.where(causal, gcol[c] - grow[c], -jnp.inf)) for c in chains]
    e = _unit_lower_inverse_minus_identity(
        [jnp.where(strict, kk[head_of[c]] * bcol[c] * decay[c], 0.0) for c in chains])
    eg = [jnp.exp(gcol[c]) for c in chains]
    rhs = [jnp.concatenate([v_all[:, lanes[c]] * bcol[c], k[head_of[c]] * (bcol[c] * eg[c])], axis=1)
           for c in chains]
    sol = [rhs[c] + _bdot(e[c], rhs[c]) for c in chains]
    qk = [jnp.where(causal, qk_raw[head_of[c]] * decay[c], 0.0) for c in chains]
    qk_sol = [_bdot(qk[c], sol[c]) for c in chains]
    q_eff = [q[head_of[c]] * eg[c] - qk_sol[c][:, HEAD_DIM:] for c in chains]
    k_dec = [k[head_of[c]] * jnp.exp(glast[c] - gcol[c]) for c in chains]
    g_chunk = [jnp.exp(glast[c]) for c in chains]
    state = [state_ref[c] for c in chains]
    for j in range(ts // CHUNK):
        rows = slice(j * CHUNK, (j + 1) * CHUNK)
        kt_sol = [_bdot_tn(k_dec[c][rows], sol[c][rows]) for c in chains]
        prod = [_bdot(jnp.concatenate([kt_sol[c][:, HEAD_DIM:], q_eff[c][rows]], axis=0), state[c])
                for c in chains]
        state = [state[c] * g_chunk[c][j * CHUNK:j * CHUNK + 1] - prod[c][:HEAD_DIM] + kt_sol[c][:, :HEAD_DIM]
                 for c in chains]
        for c in chains:
            o = prod[c][HEAD_DIM:] + qk_sol[c][rows, :HEAD_DIM]
            o = o * lax.rsqrt(jnp.mean(o * o, axis=-1, keepdims=True) + NORM_EPS) * nw
            o_ref[rows, lanes[c]] = (o * _silu(z_ref[rows, lanes[c]])).astype(o_ref.dtype)
    for c in chains:
        state_ref[c] = state[c]


GDN_TILE = 128
GDN_HEADS_PER_STEP = 4


def gdn_recurrence(proj, conv_w, gcum, beta, norm_w, key_dim, value_dim, ts=GDN_TILE, nh=GDN_HEADS_PER_STEP):
    s = proj.shape[0]
    hq = key_dim // HEAD_DIM
    hv = value_dim // HEAD_DIM
    rep = hv // hq
    nh = min(nh, hq)
    ts = min(ts, s)
    assert hq % nh == 0 and s % ts == 0
    qw = nh * HEAD_DIM
    vw = nh * rep * HEAD_DIM
    gcol = gcum.reshape(s, hq, rep).transpose(1, 0, 2)
    bcol = beta.reshape(s, hq, rep).transpose(1, 0, 2)
    grow = gcum.T.reshape(hq, rep, s)
    k_off = key_dim // qw
    v_off = 2 * key_dim // vw
    z_off = (2 * key_dim + value_dim) // vw
    return pl.pallas_call(
        _gdn_kernel,
        grid=(hq // nh, s // ts),
        in_specs=[pl.BlockSpec((ts, qw), lambda h, i: (i, h)),
                  pl.BlockSpec((ts, qw), lambda h, i: (i, k_off + h)),
                  pl.BlockSpec((ts, vw), lambda h, i: (i, v_off + h)),
                  pl.BlockSpec((ts, vw), lambda h, i: (i, z_off + h)),
                  pl.BlockSpec((CONV_WIDTH, qw), lambda h, i: (0, h)),
                  pl.BlockSpec((CONV_WIDTH, qw), lambda h, i: (0, k_off + h)),
                  pl.BlockSpec((CONV_WIDTH, vw), lambda h, i: (0, v_off + h)),
                  pl.BlockSpec((nh, ts, rep), lambda h, i: (h, i, 0)),
                  pl.BlockSpec((nh, ts, rep), lambda h, i: (h, i, 0)),
                  pl.BlockSpec((nh, rep, ts), lambda h, i: (h, 0, i)),
                  pl.BlockSpec((1, HEAD_DIM), lambda h, i: (0, 0))],
        out_specs=pl.BlockSpec((ts, vw), lambda h, i: (i, h)),
        out_shape=jax.ShapeDtypeStruct((s, value_dim), BF16),
        scratch_shapes=[pltpu.VMEM((nh * rep, HEAD_DIM, HEAD_DIM), F32),
                        pltpu.VMEM((ts + CONV_HALO, qw), F32),
                        pltpu.VMEM((ts + CONV_HALO, qw), F32),
                        pltpu.VMEM((ts + CONV_HALO, vw), F32)],
        compiler_params=_params("parallel", "arbitrary"),
        name="gdn_recurrence",
    )(proj, proj, proj, proj, conv_w, conv_w, conv_w, gcol, bcol, grow, norm_w.reshape(1, HEAD_DIM))


def gated_deltanet_mixer(u, w_in, conv_w, a_log, dt_bias, norm_w, w_out):
    hv = a_log.shape[0]
    value_dim = hv * HEAD_DIM
    key_dim = (w_in.shape[1] - 2 * value_dim - 2 * hv) // 2
    n_main = 2 * key_dim + 2 * value_dim
    proj = matmul(u, w_in[:, :n_main].astype(BF16), name="gdn_in_proj")
    ab = matmul(u, w_in[:, n_main:].astype(BF16), name="gdn_gate_proj")
    gcum, beta = gdn_gates(ab, a_log, dt_bias)
    o = gdn_recurrence(proj, conv_w, gcum, beta, norm_w, key_dim, value_dim)
    return matmul(o, w_out.astype(BF16), tn=512, name="gdn_out_proj")


def _block_rows(x, sub, offset):
    n = x.shape[0] // sub
    x3 = x.reshape(n, sub, x.shape[1])
    return jnp.broadcast_to(x3[:, offset:offset + 1, :], x3.shape).reshape(x.shape)


def _gla_intra_chunk(q, k, b):
    row = lax.broadcasted_iota(jnp.int32, (CHUNK, CHUNK), 0)
    col = lax.broadcasted_iota(jnp.int32, (CHUNK, CHUNK), 1)
    att = jnp.zeros((CHUNK, CHUNK), F32)
    size = CHUNK // 2
    while size >= SUB:
        b_row_ref = _block_rows(b, size, 0)
        nxt = jnp.concatenate([b_row_ref[size:], jnp.broadcast_to(b[CHUNK - 1:CHUNK], (size, b.shape[1]))], axis=0)
        qs = q * jnp.exp(b - b_row_ref)
        ks = k * jnp.exp(nxt - b)
        pair = ((row // size) % 2 == 1) & ((row // size) == (col // size) + 1)
        att = att + jnp.where(pair, _bdot_nt(qs, ks), 0.0)
        size //= 2
    row_in = lax.broadcasted_iota(jnp.int32, (CHUNK, 1), 0) % SUB
    same_sub = (row // SUB) == (col // SUB)
    for j in range(SUB):
        bj = _block_rows(b, SUB, j)
        kj = _block_rows(k, SUB, j)
        e = jnp.exp(jnp.where(row_in >= j, b - bj, -jnp.inf))
        s = jnp.sum(q * e * kj, axis=-1, keepdims=True)
        att = jnp.where(same_sub & ((col % SUB) == j), s, att)
    return att


def _hgrn_lower_bound(lbl_ref, layer):
    depth = lbl_ref.shape[0]
    rows = [lbl_ref[i:i + 1, :] for i in range(depth)]
    m = functools.reduce(jnp.maximum, rows)
    ex = [jnp.exp(r - m) for r in rows]
    total = functools.reduce(lambda x, y: x + y, ex)
    acc = jnp.zeros_like(m)
    for i in range(1, layer + 1):
        acc = acc + ex[i] / total
    return acc


def _hgrn_kernel(q_ref, f_ref, i_ref, og_ref, lbl_ref, nw_ref, o_ref, state_ref, *, layer):
    ts = q_ref.shape[0]

    @pl.when(pl.program_id(1) == 0)
    def _():
        state_ref[...] = jnp.zeros(state_ref.shape, F32)

    lb = _hgrn_lower_bound(lbl_ref, layer)
    f = f_ref[...]
    log_sig = -_softplus(-f)
    t0 = jnp.log(lb)
    t1 = jnp.log1p(-lb) + log_sig
    log_f = jnp.maximum(t0, t1) + jnp.log1p(jnp.exp(-jnp.abs(t0 - t1)))
    q = _silu(q_ref[...])
    k = (1.0 - lb) * _sigmoid(-f)
    row = lax.broadcasted_iota(jnp.int32, (CHUNK, CHUNK), 0)
    col = lax.broadcasted_iota(jnp.int32, (CHUNK, CHUNK), 1)
    tri = (row >= col).astype(F32)
    nw = nw_ref[...]
    nh = q.shape[1] // HEAD_DIM
    heads = range(nh)
    units = [(h, j) for j in range(ts // CHUNK) for h in heads]
    view = {(h, j): (slice(j * CHUNK, (j + 1) * CHUNK), slice(h * HEAD_DIM, (h + 1) * HEAD_DIM)) for h, j in units}
    b = {u: jnp.dot(tri, log_f[view[u]], preferred_element_type=F32, precision=lax.Precision.HIGHEST)
         for u in units}
    att = {u: _gla_intra_chunk(q[view[u]], k[view[u]], b[u]) for u in units}
    o_intra = {u: _bdot(att[u], i_ref[view[u]]) for u in units}
    q_dec = {u: q[view[u]] * jnp.exp(b[u]) for u in units}
    k_dec = {u: k[view[u]] * jnp.exp(b[u][CHUNK - 1:CHUNK, :] - b[u]) for u in units}
    kv = {u: _bdot_tn(i_ref[view[u]], k_dec[u]) for u in units}
    state_t = [state_ref[h] for h in heads]
    for j in range(ts // CHUNK):
        o = [o_intra[h, j] + _bdot_nt(q_dec[h, j], state_t[h]) for h in heads]
        state_t = [state_t[h] * jnp.exp(b[h, j][CHUNK - 1:CHUNK, :]) + kv[h, j] for h in heads]
        for h in heads:
            oh = o[h] * lax.rsqrt(jnp.mean(o[h] * o[h], axis=-1, keepdims=True) + NORM_EPS) * nw
            o_ref[view[h, j]] = (oh * _silu(og_ref[view[h, j]])).astype(o_ref.dtype)
    for h in heads:
        state_ref[h] = state_t[h]


HGRN_TILE = 256
HGRN_HEADS_PER_STEP = 2


def hgrn_recurrence(proj, lb_logits, layer, norm_w, ts=HGRN_TILE, nh=HGRN_HEADS_PER_STEP):
    s, n = proj.shape
    d = n // 4
    heads = d // HEAD_DIM
    ts = min(ts, s)
    nh = min(nh, heads)
    assert heads % nh == 0 and s % ts == 0
    depth = lb_logits.shape[0]
    groups = heads // nh
    width = nh * HEAD_DIM

    def col(off):
        return pl.BlockSpec((ts, width), lambda h, i: (i, off * groups + h))

    return pl.pallas_call(
        functools.partial(_hgrn_kernel, layer=layer),
        grid=(groups, s // ts),
        in_specs=[col(0), col(1), col(2), col(3),
                  pl.BlockSpec((depth, width), lambda h, i: (0, h)),
                  pl.BlockSpec((1, HEAD_DIM), lambda h, i: (0, 0))],
        out_specs=pl.BlockSpec((ts, width), lambda h, i: (i, h)),
        out_shape=jax.ShapeDtypeStruct((s, d), BF16),
        scratch_shapes=[pltpu.VMEM((nh, HEAD_DIM, HEAD_DIM), F32)],
        compiler_params=_params("parallel", "arbitrary"),
        name="hgrn_recurrence",
    )(proj, proj, proj, proj, lb_logits, norm_w.reshape(1, HEAD_DIM))


def hgrn2_mixer(u, w_in, lb_logits, layer, norm_w, w_out):
    proj = matmul(u, w_in.astype(BF16), name="hgrn_in_proj")
    o = hgrn_recurrence(proj, lb_logits, layer, norm_w)
    return matmul(o, w_out.astype(BF16), name="hgrn_out_proj")


ROUTER_LANES = 128
LANE_SENTINEL = 1 << 20


def _router_kernel(h_ref, mod_ref, w_ref, eid_ref, gate_ref, *, shift_row, scale_row, groups, epg_shift):
    shift = mod_ref[shift_row:shift_row + 1, :]
    scale = mod_ref[scale_row:scale_row + 1, :]
    u = h_ref[...] * (1.0 + scale) + shift
    logits = jnp.dot(u, w_ref[...], preferred_element_type=F32, precision=lax.Precision.HIGHEST)
    lane = lax.broadcasted_iota(jnp.int32, logits.shape, 1)
    n_exp = groups << epg_shift

    def first_argmax(vals):
        m = jnp.max(vals, axis=-1, keepdims=True)
        idx = jnp.min(jnp.where(vals == m, lane, LANE_SENTINEL), axis=-1, keepdims=True)
        return m, idx

    is_group = lane < groups
    gmax, grp = first_argmax(jnp.where(is_group, logits, -jnp.inf))
    p_grp = 1.0 / jnp.sum(jnp.where(is_group, jnp.exp(logits - gmax), 0.0), axis=-1, keepdims=True)
    e_lane = lane - groups
    in_grp = (e_lane >= 0) & (e_lane < n_exp) & (lax.shift_right_arithmetic(e_lane, epg_shift) == grp)
    el = jnp.where(in_grp, logits, -jnp.inf)
    m1, i1 = first_argmax(el)
    z = jnp.sum(jnp.where(in_grp, jnp.exp(logits - m1), 0.0), axis=-1, keepdims=True)
    m2, i2 = first_argmax(jnp.where(lane == i1, -jnp.inf, el))
    p1 = 1.0 / z
    p2 = jnp.exp(m2 - m1) / z
    denom = p1 + p2
    eid_ref[...] = jnp.where(lane == 0, i1 - groups, jnp.where(lane == 1, i2 - groups, 0))
    gate_ref[...] = jnp.where(lane == 0, p_grp * p1 / denom, jnp.where(lane == 1, p_grp * p2 / denom, 0.0))


def moe_router(h, mod, shift_row, scale_row, w_group, w_expert, tm=256):
    s, d = h.shape
    groups = w_group.shape[1]
    n_exp = w_expert.shape[1]
    epg = n_exp // groups
    assert epg & (epg - 1) == 0 and groups + n_exp <= ROUTER_LANES
    w = jnp.concatenate([w_group, w_expert, jnp.zeros((d, ROUTER_LANES - groups - n_exp), F32)], axis=1)
    out = pl.BlockSpec((tm, ROUTER_LANES), lambda i: (i, 0))
    eid, gate = pl.pallas_call(
        functools.partial(_router_kernel, shift_row=shift_row, scale_row=scale_row, groups=groups,
                          epg_shift=epg.bit_length() - 1),
        grid=(s // tm,),
        in_specs=[pl.BlockSpec((tm, d), lambda i: (i, 0)),
                  pl.BlockSpec(mod.shape, lambda i: (0, 0)),
                  pl.BlockSpec((d, ROUTER_LANES), lambda i: (0, 0))],
        out_specs=[out, out],
        out_shape=[jax.ShapeDtypeStruct((s, ROUTER_LANES), jnp.int32), jax.ShapeDtypeStruct((s, ROUTER_LANES), F32)],
        compiler_params=_params("parallel"),
        name="moe_router",
    )(h, mod, w)
    return eid[:, :MOE_TOP_K], gate[:, :MOE_TOP_K]


def _dispatch_plan(eid, n_exp):
    s = eid.shape[0]
    a = s * MOE_TOP_K
    e_flat = eid.reshape(a)
    onehot = (e_flat[:, None] == jnp.arange(n_exp, dtype=jnp.int32)[None, :]).astype(jnp.int32)
    csum = jnp.cumsum(onehot, axis=0)
    rank = jnp.sum(onehot * csum, axis=1) - 1
    counts = csum[-1]
    padded = (counts + MOE_ROWS - 1) // MOE_ROWS * MOE_ROWS
    pad_end = jnp.cumsum(padded)
    pad_start = pad_end - padded
    dest = pad_start[e_flat] + rank
    n_blk = -(-a // MOE_ROWS) + n_exp
    tok_buf = jnp.zeros((n_blk * MOE_ROWS,), jnp.int32).at[dest].set(jnp.arange(a, dtype=jnp.int32) // MOE_TOP_K)
    blk_start = jnp.arange(n_blk, dtype=jnp.int32) * MOE_ROWS
    blk_e = jnp.minimum(jnp.searchsorted(pad_end, blk_start, side="right"), n_exp - 1).astype(jnp.int32)
    blk_used = (blk_start < pad_end[-1]).astype(jnp.int32)
    return tok_buf, blk_e, blk_used, dest.reshape(s, MOE_TOP_K)


def _for_rows(n, fn):
    def body(r, carry):
        fn(r)
        return carry
    lax.fori_loop(0, n, body, 0)


def _moe_ffn_kernel(blk_e_ref, blk_used_ref, tok_cur, tok_next, up_hbm, wg_ref, wu_ref, wd_ref, y_ref,
                    xbuf, sem, wg_bf, wu_bf, wd_bf):
    b = pl.program_id(0)
    n = pl.num_programs(0)
    slot = b % 2
    used = blk_used_ref[b] > 0
    next_used = blk_used_ref[jnp.minimum(b + 1, n - 1)] > 0

    def row_copy(tok_ref, to_slot, r):
        return pltpu.make_async_copy(up_hbm.at[tok_ref[0, 0, r]], xbuf.at[to_slot, r], sem.at[to_slot])

    @pl.when((b == 0) & used)
    def _():
        _for_rows(MOE_ROWS, lambda r: row_copy(tok_cur, 0, r).start())

    @pl.when((b + 1 < n) & next_used)
    def _():
        _for_rows(MOE_ROWS, lambda r: row_copy(tok_next, 1 - slot, r).start())

    @pl.when(used & ((b == 0) | (blk_e_ref[b] != blk_e_ref[jnp.maximum(b - 1, 0)])))
    def _():
        wg_bf[...] = wg_ref[0, 0].astype(BF16)
        wu_bf[...] = wu_ref[0, 0].astype(BF16)
        wd_bf[...] = wd_ref[0, 0].astype(BF16)

    @pl.when(used)
    def _():
        _for_rows(MOE_ROWS, lambda r: row_copy(tok_cur, slot, r).wait())
        halves = [_unpack_bf16_halves(xbuf[slot, :, j, :]) for j in range(xbuf.shape[2])]
        x = jnp.concatenate([lo for lo, _ in halves] + [hi for _, hi in halves], axis=1)
        g = jnp.dot(x, wg_bf[...], preferred_element_type=F32)
        up = jnp.dot(x, wu_bf[...], preferred_element_type=F32)
        hid = (_silu(g) * up).astype(BF16)
        y = jnp.dot(hid, wd_bf[...], preferred_element_type=F32)
        for j in range(y_ref.shape[1]):
            y_ref[:, j, :] = y[:, j * HEAD_DIM:(j + 1) * HEAD_DIM]

    @pl.when(jnp.logical_not(used))
    def _():
        y_ref[...] = jnp.zeros(y_ref.shape, F32)


def moe_ffn(up, tok_buf, blk_e, blk_used, w_gate, w_up, w_down, layer):
    p = tok_buf.shape[0]
    d, f = w_gate.shape[-2:]
    n_blk = p // MOE_ROWS
    slabs_in = up.shape[1]
    tok3 = tok_buf.reshape(n_blk, 1, MOE_ROWS)
    grid_spec = pltpu.PrefetchScalarGridSpec(
        num_scalar_prefetch=2,
        grid=(n_blk,),
        in_specs=[pl.BlockSpec((1, 1, MOE_ROWS), lambda b, e, u: (b, 0, 0), memory_space=pltpu.SMEM),
                  pl.BlockSpec((1, 1, MOE_ROWS), lambda b, e, u: (jnp.minimum(b + 1, n_blk - 1), 0, 0),
                               memory_space=pltpu.SMEM),
                  pl.BlockSpec(memory_space=pl.ANY),
                  pl.BlockSpec((1, 1, d, f), lambda b, e, u: (layer, e[b], 0, 0)),
                  pl.BlockSpec((1, 1, d, f), lambda b, e, u: (layer, e[b], 0, 0)),
                  pl.BlockSpec((1, 1, f, d), lambda b, e, u: (layer, e[b], 0, 0))],
        out_specs=pl.BlockSpec((MOE_ROWS, d // HEAD_DIM, HEAD_DIM), lambda b, e, u: (b, 0, 0)),
        scratch_shapes=[pltpu.VMEM((2, MOE_ROWS, slabs_in, HEAD_DIM), jnp.uint32), pltpu.SemaphoreType.DMA((2,)),
                        pltpu.VMEM((d, f), BF16), pltpu.VMEM((d, f), BF16), pltpu.VMEM((f, d), BF16)],
    )
    return pl.pallas_call(
        _moe_ffn_kernel,
        grid_spec=grid_spec,
        out_shape=jax.ShapeDtypeStruct((p, d // HEAD_DIM, HEAD_DIM), F32),
        compiler_params=_params("arbitrary"),
        name="moe_ffn",
    )(blk_e, blk_used, tok3, tok3, up, w_gate, w_up, w_down)


def _combine_kernel(pos_cur, pos_next, h_ref, gw_ref, mod_ref, lnw_ref, lnb_ref, ys_hbm, *rest,
                    gate_row, shift_row, scale_row, alpha):
    out_refs, (buf, sem) = rest[:-2], rest[-2:]
    tm = h_ref.shape[0]
    i = pl.program_id(0)
    n = pl.num_programs(0)
    slot = i % 2

    def row_copy(pos_ref, to_slot, k, t):
        return pltpu.make_async_copy(ys_hbm.at[pos_ref[0, 0, k * tm + t]], buf.at[to_slot, k, t], sem.at[to_slot])

    def for_all_rows(fn):
        for k in range(MOE_TOP_K):
            _for_rows(tm, functools.partial(fn, k))

    @pl.when(i == 0)
    def _():
        for_all_rows(lambda k, t: row_copy(pos_cur, 0, k, t).start())

    @pl.when(i + 1 < n)
    def _():
        for_all_rows(lambda k, t: row_copy(pos_next, 1 - slot, k, t).start())

    for_all_rows(lambda k, t: row_copy(pos_cur, slot, k, t).wait())

    pieces = []
    for j in range(buf.shape[3]):
        piece = gw_ref[:, 0:1] * buf[slot, 0, :, j, :]
        for k in range(1, MOE_TOP_K):
            piece = piece + gw_ref[:, k:k + 1] * buf[slot, k, :, j, :]
        pieces.append(piece)
    y = jnp.concatenate(pieces, axis=1)
    gate = mod_ref[gate_row:gate_row + 1, :]
    hn = _residual_norm(h_ref[...], y, gate, lnw_ref[...], lnb_ref[...], alpha)
    out_refs[0][...] = hn
    if shift_row is not None:
        shift = mod_ref[shift_row:shift_row + 1, :]
        scale = mod_ref[scale_row:scale_row + 1, :]
        out_refs[1][...] = (hn * (1.0 + scale) + shift).astype(BF16)


def moe_combine_norm(h, ys, pos, gate_w, mod, lnw, lnb, gate_row, shift_row, scale_row, alpha, tm=128):
    s, d = h.shape
    tm = min(tm, s)
    n = s // tm
    pos_tiles = pos.reshape(n, tm, MOE_TOP_K).transpose(0, 2, 1).reshape(n, 1, MOE_TOP_K * tm)
    row = pl.BlockSpec((tm, d), lambda i: (i, 0))
    vec = pl.BlockSpec((1, d), lambda i: (0, 0))
    with_u = shift_row is not None
    out_shape = [jax.ShapeDtypeStruct((s, d), F32)] + ([jax.ShapeDtypeStruct((s, d), BF16)] if with_u else [])
    outs = pl.pallas_call(
        functools.partial(_combine_kernel, gate_row=gate_row, shift_row=shift_row, scale_row=scale_row, alpha=alpha),
        grid=(n,),
        in_specs=[pl.BlockSpec((1, 1, MOE_TOP_K * tm), lambda i: (i, 0, 0), memory_space=pltpu.SMEM),
                  pl.BlockSpec((1, 1, MOE_TOP_K * tm), lambda i: (jnp.minimum(i + 1, n - 1), 0, 0),
                               memory_space=pltpu.SMEM),
                  row,
                  pl.BlockSpec((tm, MOE_TOP_K), lambda i: (i, 0)),
                  pl.BlockSpec(mod.shape, lambda i: (0, 0)), vec, vec,
                  pl.BlockSpec(memory_space=pl.ANY)],
        out_specs=[row] * len(out_shape),
        out_shape=out_shape,
        scratch_shapes=[pltpu.VMEM((2, MOE_TOP_K, tm) + ys.shape[1:], F32), pltpu.SemaphoreType.DMA((2,))],
        compiler_params=_params("arbitrary"),
        name="moe_combine_norm",
    )(pos_tiles, pos_tiles, h, gate_w, mod, lnw.reshape(1, d), lnb.reshape(1, d), ys)
    return (outs[0], outs[1]) if with_u else (outs[0], None)


def moe_sublayer(h, up, mod, lnw, lnb, w_group, w_expert, w_gate, w_up, w_down, layer,
                 cur_shift_row, cur_scale_row, gate_row, shift_row, scale_row, alpha):
    eid, gate_w = moe_router(h, mod, cur_shift_row, cur_scale_row, w_group, w_expert)
    tok_buf, blk_e, blk_used, pos = _dispatch_plan(eid, w_expert.shape[1])
    ys = moe_ffn(up, tok_buf, blk_e, blk_used, w_gate, w_up, w_down, layer)
    return moe_combine_norm(h, ys, pos, gate_w, mod, lnw, lnb, gate_row, shift_row, scale_row, alpha)


def kernel(x, c, ada_w, ada_b, ln_w, ln_b, gdn_w_in, gdn_conv_w, gdn_a_log, gdn_dt_bias, gdn_norm_w, gdn_w_out,
           hgrn_w_in, hgrn_lb_logits, hgrn_norm_w, hgrn_w_out, moe_w_group, moe_w_expert, moe_w_gate, moe_w_up,
           moe_w_down):
    batch, s, d = x.shape
    assert batch == 1, "one sequence per call"
    depth = ada_w.shape[0]
    alpha = (2 * depth) ** 0.25
    mod = ada_modulation(c, ada_w, ada_b)
    h = x.reshape(s, d)
    u = modulate(h, mod, 0, 1)
    for layer in range(depth):
        base = 6 * layer
        j = layer // 2
        if layer % 2 == 0:
            y = gated_deltanet_mixer(u, gdn_w_in[j], gdn_conv_w[j], gdn_a_log[j], gdn_dt_bias[j], gdn_norm_w[j],
                                     gdn_w_out[j])
        else:
            y = hgrn2_mixer(u, hgrn_w_in[j], hgrn_lb_logits, layer, hgrn_norm_w[j], hgrn_w_out[j])
        h, u = residual_norm(h, y, mod, ln_w[layer, 0], ln_b[layer, 0], base + 2, base + 3, base + 4, alpha)
        last = layer + 1 == depth
        h, u = moe_sublayer(h, u, mod, ln_w[layer, 1], ln_b[layer, 1], moe_w_group[layer], moe_w_expert[layer],
                            moe_w_gate, moe_w_up, moe_w_down, layer,
                            cur_shift_row=base + 3, cur_scale_row=base + 4, gate_row=base + 5,
                            shift_row=None if last else base + 6, scale_row=None if last else base + 7, alpha=alpha)
    return h.reshape(batch, s, d)
```

```python
import functools

import jax
import jax.numpy as jnp
from jax import lax
from jax.experimental import pallas as pl
from jax.experimental.pallas import tpu as pltpu

F32 = jnp.float32
BF16 = jnp.bfloat16

HEAD_DIM = 128
CHUNK = 64
SUB = 16
CONV_WIDTH = 4
CONV_HALO = 8
NORM_EPS = 1e-6
MOE_TOP_K = 2
MOE_ROWS = 256
V7X_VMEM_BYTES = 64 * 1024 * 1024
VMEM_LIMIT = V7X_VMEM_BYTES - 8 * 1024 * 1024


def _params(*sem, vmem=VMEM_LIMIT):
    return pltpu.CompilerParams(dimension_semantics=sem, vmem_limit_bytes=vmem)


def _sigmoid(x):
    return 1.0 / (1.0 + jnp.exp(-x))


def _silu(x):
    return x * _sigmoid(x)


def _softplus(x):
    return jnp.maximum(x, 0.0) + jnp.log1p(jnp.exp(-jnp.abs(x)))


def _bdot(a, b):
    return jnp.dot(a.astype(BF16), b.astype(BF16), preferred_element_type=F32)


def _bdot_nt(a, b):
    return lax.dot_general(a.astype(BF16), b.astype(BF16), (((1,), (1,)), ((), ())),
                           preferred_element_type=F32)


def _bdot_tn(a, b):
    return lax.dot_general(a.astype(BF16), b.astype(BF16), (((0,), (0,)), ((), ())),
                           preferred_element_type=F32)


def _ada_kernel(c_ref, w_ref, b_ref, o_ref):
    cond = _silu(c_ref[...])
    o_ref[0] = jnp.sum(cond * w_ref[0], axis=0, keepdims=True) + b_ref[0]


def ada_modulation(c, ada_w, ada_b, tn=512):
    depth, d, n = ada_w.shape
    out = pl.pallas_call(
        _ada_kernel,
        grid=(depth, n // tn),
        in_specs=[pl.BlockSpec((d, 1), lambda l, j: (0, 0)),
                  pl.BlockSpec((1, d, tn), lambda l, j: (l, 0, j)),
                  pl.BlockSpec((1, 1, tn), lambda l, j: (l, 0, j))],
        out_specs=pl.BlockSpec((1, 1, tn), lambda l, j: (l, 0, j)),
        out_shape=jax.ShapeDtypeStruct((depth, 1, n), F32),
        compiler_params=_params("parallel", "parallel"),
        name="ada_modulation",
    )(c.reshape(d, 1), ada_w, ada_b.reshape(depth, 1, n))
    return out.reshape(depth * 6, d)


def _modulate_kernel(x_ref, mod_ref, u_ref, *, shift_row, scale_row):
    shift = mod_ref[shift_row:shift_row + 1, :]
    scale = mod_ref[scale_row:scale_row + 1, :]
    u_ref[...] = (x_ref[...] * (1.0 + scale) + shift).astype(u_ref.dtype)


def modulate(x, mod, shift_row, scale_row, tm=256):
    s, d = x.shape
    return pl.pallas_call(
        functools.partial(_modulate_kernel, shift_row=shift_row, scale_row=scale_row),
        grid=(s // tm,),
        in_specs=[pl.BlockSpec((tm, d), lambda i: (i, 0)),
                  pl.BlockSpec(mod.shape, lambda i: (0, 0))],
        out_specs=pl.BlockSpec((tm, d), lambda i: (i, 0)),
        out_shape=jax.ShapeDtypeStruct((s, d), BF16),
        compiler_params=_params("parallel"),
        name="modulate",
    )(x, mod)


def _mm_kernel(x_ref, w_ref, o_ref):
    o_ref[...] = jnp.dot(x_ref[...], w_ref[...], preferred_element_type=F32).astype(o_ref.dtype)


def matmul(x, w, tm=512, tn=1024, out_dtype=F32, name="matmul"):
    m, k = x.shape
    _, n = w.shape
    tm, tn = min(tm, m), min(tn, n)
    while n % tn:
        tn -= HEAD_DIM
    assert m % tm == 0 and tn > 0, (m, n, tm, tn)
    return pl.pallas_call(
        _mm_kernel,
        grid=(n // tn, m // tm),
        in_specs=[pl.BlockSpec((tm, k), lambda j, i: (i, 0)),
                  pl.BlockSpec((k, tn), lambda j, i: (0, j))],
        out_specs=pl.BlockSpec((tm, tn), lambda j, i: (i, j)),
        out_shape=jax.ShapeDtypeStruct((m, n), out_dtype),
        compiler_params=_params("parallel", "parallel"),
        name=name,
    )(x, w)


def _residual_norm(h, y, gate, lnw, lnb, alpha):
    v = alpha * h + (1.0 + gate) * y
    mu = jnp.mean(v, axis=-1, keepdims=True)
    dv = v - mu
    var = jnp.mean(dv * dv, axis=-1, keepdims=True)
    return dv * lax.rsqrt(var + NORM_EPS) * lnw + lnb


HI_HALF_MASK = 0xFFFF0000


def _pack_bf16_halves(u):
    half = u.shape[1] // 2
    lo = pltpu.bitcast(u[:, :half].astype(BF16).astype(F32), jnp.uint32)
    hi = pltpu.bitcast(u[:, half:].astype(BF16).astype(F32), jnp.uint32)
    return (lo >> 16) | (hi & jnp.uint32(HI_HALF_MASK))


def _unpack_bf16_halves(words):
    lo = pltpu.bitcast(words << 16, F32).astype(BF16)
    hi = pltpu.bitcast(words & jnp.uint32(HI_HALF_MASK), F32).astype(BF16)
    return lo, hi


def _ln_kernel(h_ref, y_ref, mod_ref, lnw_ref, lnb_ref, h_out, up_out, *, gate_row, shift_row, scale_row, alpha):
    gate = mod_ref[gate_row:gate_row + 1, :]
    hn = _residual_norm(h_ref[...], y_ref[...], gate, lnw_ref[...], lnb_ref[...], alpha)
    h_out[...] = hn
    shift = mod_ref[shift_row:shift_row + 1, :]
    scale = mod_ref[scale_row:scale_row + 1, :]
    words = _pack_bf16_halves(hn * (1.0 + scale) + shift)
    for j in range(up_out.shape[1]):
        up_out[:, j, :] = words[:, j * HEAD_DIM:(j + 1) * HEAD_DIM]


def residual_norm(h, y, mod, lnw, lnb, gate_row, shift_row, scale_row, alpha, tm=256):
    s, d = h.shape
    row = pl.BlockSpec((tm, d), lambda i: (i, 0))
    vec = pl.BlockSpec((1, d), lambda i: (0, 0))
    slabs = d // (2 * HEAD_DIM)
    return pl.pallas_call(
        functools.partial(_ln_kernel, gate_row=gate_row, shift_row=shift_row, scale_row=scale_row, alpha=alpha),
        grid=(s // tm,),
        in_specs=[row, row, pl.BlockSpec(mod.shape, lambda i: (0, 0)), vec, vec],
        out_specs=[row, pl.BlockSpec((tm, slabs, HEAD_DIM), lambda i: (i, 0, 0))],
        out_shape=[jax.ShapeDtypeStruct((s, d), F32), jax.ShapeDtypeStruct((s, slabs, HEAD_DIM), jnp.uint32)],
        compiler_params=_params("parallel"),
        name="residual_norm",
    )(h, y, mod, lnw.reshape(1, d), lnb.reshape(1, d))


def _gdn_gate_kernel(ab_ref, alog_ref, dtb_ref, gcum_ref, beta_ref):
    hv = alog_ref.shape[-1]
    a = ab_ref[:, :hv]
    b = ab_ref[:, hv:]
    g = -jnp.exp(alog_ref[...]) * _softplus(a + dtb_ref[...])
    beta_ref[...] = _sigmoid(b)
    tm = g.shape[0]
    row = lax.broadcasted_iota(jnp.int32, (CHUNK, CHUNK), 0)
    col = lax.broadcasted_iota(jnp.int32, (CHUNK, CHUNK), 1)
    tri = (row >= col).astype(F32)
    for c in range(tm // CHUNK):
        gcum_ref[c * CHUNK:(c + 1) * CHUNK, :] = jnp.dot(
            tri, g[c * CHUNK:(c + 1) * CHUNK, :], preferred_element_type=F32, precision=lax.Precision.HIGHEST)


def gdn_gates(ab, a_log, dt_bias, tm=512):
    s, two_hv = ab.shape
    hv = two_hv // 2
    tm = min(tm, s)
    vec = pl.BlockSpec((1, hv), lambda i: (0, 0))
    out = pl.BlockSpec((tm, hv), lambda i: (i, 0))
    return pl.pallas_call(
        _gdn_gate_kernel,
        grid=(s // tm,),
        in_specs=[pl.BlockSpec((tm, two_hv), lambda i: (i, 0)), vec, vec],
        out_specs=[out, out],
        out_shape=[jax.ShapeDtypeStruct((s, hv), F32)] * 2,
        compiler_params=_params("parallel"),
        name="gdn_gates",
    )(ab, a_log.reshape(1, hv), dt_bias.reshape(1, hv))


def _l2norm(x):
    return x * lax.rsqrt(jnp.sum(x * x, axis=-1, keepdims=True) + NORM_EPS)


PROJ_SUB = 256


def _proj_act_kernel(x_ref, w_ref, *rest, mode):
    tn = w_ref.shape[1]
    n_sub = tn // PROJ_SUB
    if mode == "silu":
        (o_ref,) = rest
        cbufs = ()
    elif mode == "conv":
        cw_ref, o_ref, *cbufs = rest
    else:
        cw_ref, sc_ref, o_ref, *cbufs = rest
    tm = o_ref.shape[0]
    x = x_ref[...]

    if mode != "silu":
        @pl.when(pl.program_id(1) == 0)
        def _():
            for cbuf in cbufs:
                cbuf[0:CONV_HALO, :] = jnp.zeros((CONV_HALO, PROJ_SUB), F32)

    def matmul_sub(c):
        cols = slice(c * PROJ_SUB, (c + 1) * PROJ_SUB)
        acc = jnp.dot(x, w_ref[:, cols], preferred_element_type=F32)
        if mode == "silu":
            return acc
        cbufs[c][CONV_HALO:CONV_HALO + tm, :] = acc
        return None

    def epilogue(c, acc):
        cols = slice(c * PROJ_SUB, (c + 1) * PROJ_SUB)
        if mode == "silu":
            o_ref[:, cols] = _silu(acc)
            return
        cbuf = cbufs[c]
        conv = None
        for j in range(CONV_WIDTH):
            start = CONV_HALO - (CONV_WIDTH - 1) + j
            term = cw_ref[j:j + 1, cols] * cbuf[start:start + tm, :]
            conv = term if conv is None else conv + term
        cbuf[0:CONV_HALO, :] = cbuf[tm:tm + CONV_HALO, :]
        act = _silu(conv)
        if mode == "conv":
            o_ref[:, cols] = act
            return
        for h in range(PROJ_SUB // HEAD_DIM):
            lanes = slice(c * PROJ_SUB + h * HEAD_DIM, c * PROJ_SUB + (h + 1) * HEAD_DIM)
            o_ref[:, lanes] = _l2norm(act[:, h * HEAD_DIM:(h + 1) * HEAD_DIM]) * sc_ref[:, lanes]

    pending = None
    for c in range(n_sub):
        if pending is not None:
            epilogue(*pending)
        pending = (c, matmul_sub(c))
    epilogue(*pending)


def proj_act(x, w, mode, conv_w=None, col_scale=None, tm=512, tn=1024, name="proj_act"):
    m, k = x.shape
    n = w.shape[1]
    tm, tn = min(tm, m), min(tn, n)
    assert m % tm == 0 and n % tn == 0 and tn % PROJ_SUB == 0
    col = pl.BlockSpec((1, tn), lambda j, i: (0, j))
    in_specs = [pl.BlockSpec((tm, k), lambda j, i: (i, 0)), pl.BlockSpec((k, tn), lambda j, i: (0, j))]
    args = [x, w]
    scratch = []
    if mode != "silu":
        in_specs.append(pl.BlockSpec((CONV_WIDTH, tn), lambda j, i: (0, j)))
        args.append(conv_w)
        scratch += [pltpu.VMEM((tm + CONV_HALO, PROJ_SUB), F32)] * (tn // PROJ_SUB)
    if mode == "conv_norm":
        in_specs.append(col)
        args.append(col_scale.reshape(1, n))
    return pl.pallas_call(
        functools.partial(_proj_act_kernel, mode=mode),
        grid=(n // tn, m // tm),
        in_specs=in_specs,
        out_specs=pl.BlockSpec((tm, tn), lambda j, i: (i, j)),
        out_shape=jax.ShapeDtypeStruct((m, n), F32),
        scratch_shapes=scratch,
        compiler_params=_params("arbitrary", "arbitrary"),
        name=name,
    )(*args)


def _unit_lower_inverse_minus_identity(mats):
    n = range(len(mats))
    row = lax.broadcasted_iota(jnp.int32, mats[0].shape, 0)
    col = lax.broadcasted_iota(jnp.int32, mats[0].shape, 1)
    same_sub = (row // SUB) == (col // SUB)
    b1 = [jnp.where(same_sub, -a, 0.0) for a in mats]
    bp = [_bdot(b1[i], b1[i]) for i in n]
    e = [b1[i] + bp[i] + _bdot(b1[i], bp[i]) for i in n]
    width = 4
    while width < SUB:
        bp = [_bdot(bp[i], bp[i]) for i in n]
        e = [e[i] + bp[i] + _bdot(e[i], bp[i]) for i in n]
        width *= 2
    size = SUB
    while size < CHUNK:
        lower_pair = ((row // (2 * size)) == (col // (2 * size))) & ((row // size) > (col // size))
        f = [jnp.where(lower_pair, a, 0.0) for a in mats]
        g = [f[i] + _bdot(e[i], f[i]) for i in n]
        e = [e[i] - (g[i] + _bdot(g[i], e[i])) for i in n]
        size *= 2
    return e


def _gdn_kernel(q_ref, k_ref, v_ref, z_ref, gcol_ref, bcol_ref, grow_ref, nw_ref, o_ref, state_ref):
    ts = q_ref.shape[0]

    @pl.when(pl.program_id(1) == 0)
    def _():
        state_ref[...] = jnp.zeros(state_ref.shape, F32)

    v_all = v_ref[...]
    nh = q_ref.shape[1] // HEAD_DIM
    rep = v_all.shape[1] // q_ref.shape[1]

    row = lax.broadcasted_iota(jnp.int32, (ts, ts), 0)
    col = lax.broadcasted_iota(jnp.int32, (ts, ts), 1)
    same_chunk = (row // CHUNK) == (col // CHUNK)
    causal = same_chunk & (row >= col)
    strict = same_chunk & (row > col)
    nw = nw_ref[...]

    heads = range(nh)
    chains = range(nh * rep)
    head_of = [c // rep for c in chains]
    lanes = [slice(c * HEAD_DIM, (c + 1) * HEAD_DIM) for c in chains]
    q = [q_ref[:, h * HEAD_DIM:(h + 1) * HEAD_DIM] for h in heads]
    k = [k_ref[:, h * HEAD_DIM:(h + 1) * HEAD_DIM] for h in heads]
    kk = [_bdot_nt(k[h], k[h]) for h in heads]
    qk_raw = [_bdot_nt(q[h], k[h]) for h in heads]
    gcol = [gcol_ref[c // rep, :, c % rep:c % rep + 1] for c in chains]
    bcol = [bcol_ref[c // rep, :, c % rep:c % rep + 1] for c in chains]
    grow = [grow_ref[c // rep, c % rep:c % rep + 1, :] for c in chains]
    glast = [_block_rows(gcol[c], CHUNK, CHUNK - 1) for c in chains]
    decay = [jnp.exp(jnp.where(causal, gcol[c] - grow[c], -jnp.inf)) for c in chains]
    e = _unit_lower_inverse_minus_identity(
        [jnp.where(strict, kk[head_of[c]] * bcol[c] * decay[c], 0.0) for c in chains])
    eg = [jnp.exp(gcol[c]) for c in chains]
    rhs = [jnp.concatenate([v_all[:, lanes[c]] * bcol[c], k[head_of[c]] * (bcol[c] * eg[c])], axis=1)
           for c in chains]
    sol = [rhs[c] + _bdot(e[c], rhs[c]) for c in chains]
    qk = [jnp.where(causal, qk_raw[head_of[c]] * decay[c], 0.0) for c in chains]
    qk_sol = [_bdot(qk[c], sol[c]) for c in chains]
    q_eff = [q[head_of[c]] * eg[c] - qk_sol[c][:, HEAD_DIM:] for c in chains]
    k_dec = [k[head_of[c]] * jnp.exp(glast[c] - gcol[c]) for c in chains]
    g_chunk = [jnp.exp(glast[c]) for c in chains]
    state = [state_ref[c] for c in chains]
    for j in range(ts // CHUNK):
        rows = slice(j * CHUNK, (j + 1) * CHUNK)
        kt_sol = [_bdot_tn(k_dec[c][rows], sol[c][rows]) for c in chains]
        prod = [_bdot(jnp.concatenate([kt_sol[c][:, HEAD_DIM:], q_eff[c][rows]], axis=0), state[c])
                for c in chains]
        state = [state[c] * g_chunk[c][j * CHUNK:j * CHUNK + 1] - prod[c][:HEAD_DIM] + kt_sol[c][:, :HEAD_DIM]
                 for c in chains]
        for c in chains:
            o = prod[c][HEAD_DIM:] + qk_sol[c][rows, :HEAD_DIM]
            o = o * lax.rsqrt(jnp.mean(o * o, axis=-1, keepdims=True) + NORM_EPS) * nw
            o_ref[rows, lanes[c]] = (o * z_ref[rows, lanes[c]]).astype(o_ref.dtype)
    for c in chains:
        state_ref[c] = state[c]


GDN_TILE = 128
GDN_HEADS_PER_STEP = 4


def gdn_recurrence(qk, v, z_act, gcum, beta, norm_w, ts=GDN_TILE, nh=GDN_HEADS_PER_STEP):
    s = qk.shape[0]
    key_dim = qk.shape[1] // 2
    value_dim = v.shape[1]
    hq = key_dim // HEAD_DIM
    hv = value_dim // HEAD_DIM
    rep = hv // hq
    nh = min(nh, hq)
    ts = min(ts, s)
    assert hq % nh == 0 and s % ts == 0
    qw = nh * HEAD_DIM
    vw = nh * rep * HEAD_DIM
    gcol = gcum.reshape(s, hq, rep).transpose(1, 0, 2)
    bcol = beta.reshape(s, hq, rep).transpose(1, 0, 2)
    grow = gcum.T.reshape(hq, rep, s)
    k_off = key_dim // qw
    return pl.pallas_call(
        _gdn_kernel,
        grid=(hq // nh, s // ts),
        in_specs=[pl.BlockSpec((ts, qw), lambda h, i: (i, h)),
                  pl.BlockSpec((ts, qw), lambda h, i: (i, k_off + h)),
                  pl.BlockSpec((ts, vw), lambda h, i: (i, h)),
                  pl.BlockSpec((ts, vw), lambda h, i: (i, h)),
                  pl.BlockSpec((nh, ts, rep), lambda h, i: (h, i, 0)),
                  pl.BlockSpec((nh, ts, rep), lambda h, i: (h, i, 0)),
                  pl.BlockSpec((nh, rep, ts), lambda h, i: (h, 0, i)),
                  pl.BlockSpec((1, HEAD_DIM), lambda h, i: (0, 0))],
        out_specs=pl.BlockSpec((ts, vw), lambda h, i: (i, h)),
        out_shape=jax.ShapeDtypeStruct((s, value_dim), BF16),
        scratch_shapes=[pltpu.VMEM((nh * rep, HEAD_DIM, HEAD_DIM), F32)],
        compiler_params=_params("parallel", "arbitrary"),
        name="gdn_recurrence",
    )(qk, qk, v, z_act, gcol, bcol, grow, norm_w.reshape(1, HEAD_DIM))


def gated_deltanet_mixer(u, w_in, conv_w, a_log, dt_bias, norm_w, w_out):
    hv = a_log.shape[0]
    value_dim = hv * HEAD_DIM
    key_dim = (w_in.shape[1] - 2 * value_dim - 2 * hv) // 2
    c_qk, c_v, c_z = 2 * key_dim, 2 * key_dim + value_dim, 2 * key_dim + 2 * value_dim
    q_scale = jnp.concatenate([jnp.full((key_dim,), HEAD_DIM ** -0.5, F32), jnp.ones((key_dim,), F32)])
    qk = proj_act(u, w_in[:, :c_qk].astype(BF16), "conv_norm", conv_w[:, :c_qk], q_scale, name="gdn_qk_proj")
    v = proj_act(u, w_in[:, c_qk:c_v].astype(BF16), "conv", conv_w[:, c_qk:c_v], name="gdn_v_proj")
    z_act = proj_act(u, w_in[:, c_v:c_z].astype(BF16), "silu", name="gdn_z_proj")
    ab = matmul(u, w_in[:, c_z:].astype(BF16), name="gdn_gate_proj")
    gcum, beta = gdn_gates(ab, a_log, dt_bias)
    o = gdn_recurrence(qk, v, z_act, gcum, beta, norm_w)
    return matmul(o, w_out.astype(BF16), tn=512, name="gdn_out_proj")


def _block_rows(x, sub, offset):
    n = x.shape[0] // sub
    x3 = x.reshape(n, sub, x.shape[1])
    return jnp.broadcast_to(x3[:, offset:offset + 1, :], x3.shape).reshape(x.shape)


def _gla_intra_chunk(q, k, b):
    row = lax.broadcasted_iota(jnp.int32, (CHUNK, CHUNK), 0)
    col = lax.broadcasted_iota(jnp.int32, (CHUNK, CHUNK), 1)
    att = jnp.zeros((CHUNK, CHUNK), F32)
    size = CHUNK // 2
    while size >= SUB:
        b_row_ref = _block_rows(b, size, 0)
        nxt = jnp.concatenate([b_row_ref[size:], jnp.broadcast_to(b[CHUNK - 1:CHUNK], (size, b.shape[1]))], axis=0)
        qs = q * jnp.exp(b - b_row_ref)
        ks = k * jnp.exp(nxt - b)
        pair = ((row // size) % 2 == 1) & ((row // size) == (col // size) + 1)
        att = att + jnp.where(pair, _bdot_nt(qs, ks), 0.0)
        size //= 2
    row_in = lax.broadcasted_iota(jnp.int32, (CHUNK, 1), 0) % SUB
    same_sub = (row // SUB) == (col // SUB)
    for j in range(SUB):
        bj = _block_rows(b, SUB, j)
        kj = _block_rows(k, SUB, j)
        e = jnp.exp(jnp.where(row_in >= j, b - bj, -jnp.inf))
        s = jnp.sum(q * e * kj, axis=-1, keepdims=True)
        att = jnp.where(same_sub & ((col % SUB) == j), s, att)
    return att


def _hgrn_lower_bound(lbl_ref, layer):
    depth = lbl_ref.shape[0]
    rows = [lbl_ref[i:i + 1, :] for i in range(depth)]
    m = functools.reduce(jnp.maximum, rows)
    ex = [jnp.exp(r - m) for r in rows]
    total = functools.reduce(lambda x, y: x + y, ex)
    acc = jnp.zeros_like(m)
    for i in range(1, layer + 1):
        acc = acc + ex[i] / total
    return acc


def _hgrn_kernel(q_ref, f_ref, i_ref, og_ref, lbl_ref, nw_ref, o_ref, state_ref, *, layer):
    ts = q_ref.shape[0]

    @pl.when(pl.program_id(1) == 0)
    def _():
        state_ref[...] = jnp.zeros(state_ref.shape, F32)

    lb = _hgrn_lower_bound(lbl_ref, layer)
    f = f_ref[...]
    log_sig = -_softplus(-f)
    t0 = jnp.log(lb)
    t1 = jnp.log1p(-lb) + log_sig
    log_f = jnp.maximum(t0, t1) + jnp.log1p(jnp.exp(-jnp.abs(t0 - t1)))
    q = _silu(q_ref[...])
    k = (1.0 - lb) * _sigmoid(-f)
    row = lax.broadcasted_iota(jnp.int32, (CHUNK, CHUNK), 0)
    col = lax.broadcasted_iota(jnp.int32, (CHUNK, CHUNK), 1)
    tri = (row >= col).astype(F32)
    nw = nw_ref[...]
    nh = q.shape[1] // HEAD_DIM
    heads = range(nh)
    units = [(h, j) for j in range(ts // CHUNK) for h in heads]
    view = {(h, j): (slice(j * CHUNK, (j + 1) * CHUNK), slice(h * HEAD_DIM, (h + 1) * HEAD_DIM)) for h, j in units}
    b = {u: jnp.dot(tri, log_f[view[u]], preferred_element_type=F32, precision=lax.Precision.HIGHEST)
         for u in units}
    att = {u: _gla_intra_chunk(q[view[u]], k[view[u]], b[u]) for u in units}
    o_intra = {u: _bdot(att[u], i_ref[view[u]]) for u in units}
    q_dec = {u: q[view[u]] * jnp.exp(b[u]) for u in units}
    k_dec = {u: k[view[u]] * jnp.exp(b[u][CHUNK - 1:CHUNK, :] - b[u]) for u in units}
    kv = {u: _bdot_tn(i_ref[view[u]], k_dec[u]) for u in units}
    state_t = [state_ref[h] for h in heads]
    for j in range(ts // CHUNK):
        o = [o_intra[h, j] + _bdot_nt(q_dec[h, j], state_t[h]) for h in heads]
        state_t = [state_t[h] * jnp.exp(b[h, j][CHUNK - 1:CHUNK, :]) + kv[h, j] for h in heads]
        for h in heads:
            oh = o[h] * lax.rsqrt(jnp.mean(o[h] * o[h], axis=-1, keepdims=True) + NORM_EPS) * nw
            o_ref[view[h, j]] = (oh * _silu(og_ref[view[h, j]])).astype(o_ref.dtype)
    for h in heads:
        state_ref[h] = state_t[h]


HGRN_TILE = 256
HGRN_HEADS_PER_STEP = 4


def hgrn_recurrence(proj, lb_logits, layer, norm_w, ts=HGRN_TILE, nh=HGRN_HEADS_PER_STEP):
    s, n = proj.shape
    d = n // 4
    heads = d // HEAD_DIM
    ts = min(ts, s)
    nh = min(nh, heads)
    assert heads % nh == 0 and s % ts == 0
    depth = lb_logits.shape[0]
    groups = heads // nh
    width = nh * HEAD_DIM

    def col(off):
        return pl.BlockSpec((ts, width), lambda h, i: (i, off * groups + h))

    return pl.pallas_call(
        functools.partial(_hgrn_kernel, layer=layer),
        grid=(groups, s // ts),
        in_specs=[col(0), col(1), col(2), col(3),
                  pl.BlockSpec((depth, width), lambda h, i: (0, h)),
                  pl.BlockSpec((1, HEAD_DIM), lambda h, i: (0, 0))],
        out_specs=pl.BlockSpec((ts, width), lambda h, i: (i, h)),
        out_shape=jax.ShapeDtypeStruct((s, d), BF16),
        scratch_shapes=[pltpu.VMEM((nh, HEAD_DIM, HEAD_DIM), F32)],
        compiler_params=_params("parallel", "arbitrary"),
        name="hgrn_recurrence",
    )(proj, proj, proj, proj, lb_logits, norm_w.reshape(1, HEAD_DIM))


def hgrn2_mixer(u, w_in, lb_logits, layer, norm_w, w_out):
    proj = matmul(u, w_in.astype(BF16), name="hgrn_in_proj")
    o = hgrn_recurrence(proj, lb_logits, layer, norm_w)
    return matmul(o, w_out.astype(BF16), name="hgrn_out_proj")


ROUTER_LANES = 128
LANE_SENTINEL = 1 << 20


def _router_kernel(h_ref, mod_ref, w_ref, eid_ref, gate_ref, rank_ref, count_ref, run_ref, *,
                   shift_row, scale_row, groups, epg_shift):
    @pl.when(pl.program_id(0) == 0)
    def _():
        run_ref[...] = jnp.zeros(run_ref.shape, F32)

    shift = mod_ref[shift_row:shift_row + 1, :]
    scale = mod_ref[scale_row:scale_row + 1, :]
    u = h_ref[...] * (1.0 + scale) + shift
    logits = jnp.dot(u, w_ref[...], preferred_element_type=F32, precision=lax.Precision.HIGHEST)
    lane = lax.broadcasted_iota(jnp.int32, logits.shape, 1)
    n_exp = groups << epg_shift

    def first_argmax(vals):
        m = jnp.max(vals, axis=-1, keepdims=True)
        idx = jnp.min(jnp.where(vals == m, lane, LANE_SENTINEL), axis=-1, keepdims=True)
        return m, idx

    is_group = lane < groups
    gmax, grp = first_argmax(jnp.where(is_group, logits, -jnp.inf))
    p_grp = 1.0 / jnp.sum(jnp.where(is_group, jnp.exp(logits - gmax), 0.0), axis=-1, keepdims=True)
    e_lane = lane - groups
    in_grp = (e_lane >= 0) & (e_lane < n_exp) & (lax.shift_right_arithmetic(e_lane, epg_shift) == grp)
    el = jnp.where(in_grp, logits, -jnp.inf)
    m1, i1 = first_argmax(el)
    z = jnp.sum(jnp.where(in_grp, jnp.exp(logits - m1), 0.0), axis=-1, keepdims=True)
    m2, i2 = first_argmax(jnp.where(lane == i1, -jnp.inf, el))
    p1 = 1.0 / z
    p2 = jnp.exp(m2 - m1) / z
    denom = p1 + p2
    e1, e2 = i1 - groups, i2 - groups
    eid_ref[...] = jnp.where(lane == 0, e1, jnp.where(lane == 1, e2, 0))
    gate_ref[...] = jnp.where(lane == 0, p_grp * p1 / denom, jnp.where(lane == 1, p_grp * p2 / denom, 0.0))

    tm = logits.shape[0]
    tri = (lax.broadcasted_iota(jnp.int32, (tm, tm), 0) >= lax.broadcasted_iota(jnp.int32, (tm, tm), 1)).astype(BF16)
    hot1 = (lane == e1).astype(F32)
    hot2 = (lane == e2).astype(F32)
    cum1 = jnp.dot(tri, hot1.astype(BF16), preferred_element_type=F32)
    cum2 = jnp.dot(tri, hot2.astype(BF16), preferred_element_type=F32)
    run = run_ref[...]
    after1 = run + cum1[tm - 1:tm, :]
    rank1 = jnp.sum(hot1 * (run + cum1), axis=-1, keepdims=True) - 1.0
    rank2 = jnp.sum(hot2 * (after1 + cum2), axis=-1, keepdims=True) - 1.0
    total = after1 + cum2[tm - 1:tm, :]
    run_ref[...] = total
    count_ref[...] = total.astype(jnp.int32)
    rank_ref[...] = jnp.where(lane == 0, rank1, jnp.where(lane == 1, rank2, 0.0)).astype(jnp.int32)


def moe_router(h, mod, shift_row, scale_row, w_group, w_expert, tm=256):
    s, d = h.shape
    groups = w_group.shape[1]
    n_exp = w_expert.shape[1]
    epg = n_exp // groups
    assert epg & (epg - 1) == 0 and groups + n_exp <= ROUTER_LANES
    w = jnp.concatenate([w_group, w_expert, jnp.zeros((d, ROUTER_LANES - groups - n_exp), F32)], axis=1)
    out = pl.BlockSpec((tm, ROUTER_LANES), lambda i: (i, 0))
    ints = jax.ShapeDtypeStruct((s, ROUTER_LANES), jnp.int32)
    eid, gate, rank, counts = pl.pallas_call(
        functools.partial(_router_kernel, shift_row=shift_row, scale_row=scale_row, groups=groups,
                          epg_shift=epg.bit_length() - 1),
        grid=(s // tm,),
        in_specs=[pl.BlockSpec((tm, d), lambda i: (i, 0)),
                  pl.BlockSpec(mod.shape, lambda i: (0, 0)),
                  pl.BlockSpec((d, ROUTER_LANES), lambda i: (0, 0))],
        out_specs=[out, out, out, pl.BlockSpec((1, ROUTER_LANES), lambda i: (0, 0))],
        out_shape=[ints, jax.ShapeDtypeStruct((s, ROUTER_LANES), F32), ints,
                   jax.ShapeDtypeStruct((1, ROUTER_LANES), jnp.int32)],
        scratch_shapes=[pltpu.VMEM((1, ROUTER_LANES), F32)],
        compiler_params=_params("arbitrary"),
        name="moe_router",
    )(h, mod, w)
    return eid[:, :MOE_TOP_K], gate[:, :MOE_TOP_K], rank[:, :MOE_TOP_K], counts[0, :n_exp]


def _dispatch_plan(eid, rank, counts):
    s = eid.shape[0]
    n_exp = counts.shape[0]
    a = s * MOE_TOP_K
    padded = (counts + MOE_ROWS - 1) // MOE_ROWS * MOE_ROWS
    pad_end = jnp.cumsum(padded)
    pad_start = pad_end - padded
    dest = pad_start[eid] + rank
    n_blk = -(-a // MOE_ROWS) + n_exp
    tok_buf = jnp.zeros((n_blk * MOE_ROWS,), jnp.int32).at[dest.reshape(a)].set(
        jnp.arange(a, dtype=jnp.int32) // MOE_TOP_K)
    blk_start = jnp.arange(n_blk, dtype=jnp.int32) * MOE_ROWS
    blk_e = jnp.minimum(jnp.searchsorted(pad_end, blk_start, side="right"), n_exp - 1).astype(jnp.int32)
    blk_used = (blk_start < pad_end[-1]).astype(jnp.int32)
    return tok_buf, blk_e, blk_used, dest


def _for_rows(n, fn):
    for r in range(n):
        fn(r)


def _moe_ffn_kernel(blk_e_ref, blk_used_ref, tok_cur, tok_next, up_hbm, wg_ref, wu_ref, wd_ref, y_ref,
                    xbuf, sem, w_in_bf, wd_bf, prod_ref):
    f, d = wd_bf.shape
    half = d // 2
    b = pl.program_id(0)
    n = pl.num_programs(0)
    slot = b % 2
    used = blk_used_ref[b] > 0
    next_used = blk_used_ref[jnp.minimum(b + 1, n - 1)] > 0

    def row_copy(tok_ref, to_slot, r):
        return pltpu.make_async_copy(up_hbm.at[tok_ref[0, 0, r]], xbuf.at[to_slot, r], sem.at[to_slot])

    @pl.when((b == 0) & used)
    def _():
        _for_rows(MOE_ROWS, lambda r: row_copy(tok_cur, 0, r).start())

    @pl.when((b + 1 < n) & next_used)
    def _():
        _for_rows(MOE_ROWS, lambda r: row_copy(tok_next, 1 - slot, r).start())

    @pl.when(used & ((b == 0) | (blk_e_ref[b] != blk_e_ref[jnp.maximum(b - 1, 0)])))
    def _():
        w_in_bf[:, 0 * f:1 * f] = wg_ref[0, 0, 0:half, :].astype(BF16)
        w_in_bf[:, 1 * f:2 * f] = wg_ref[0, 0, half:d, :].astype(BF16)
        w_in_bf[:, 2 * f:3 * f] = wu_ref[0, 0, 0:half, :].astype(BF16)
        w_in_bf[:, 3 * f:4 * f] = wu_ref[0, 0, half:d, :].astype(BF16)
        wd_bf[...] = wd_ref[0, 0].astype(BF16)

    @pl.when(used)
    def _():
        _for_rows(MOE_ROWS, lambda r: row_copy(tok_cur, slot, r).wait())
        x2 = jnp.concatenate([pltpu.bitcast(xbuf[slot, :, j, :], BF16) for j in range(xbuf.shape[2])], axis=1)
        prod = jnp.dot(x2, w_in_bf[...], preferred_element_type=F32)
        tiles = f // HEAD_DIM
        for t in range(4 * tiles):
            prod_ref[t] = prod[:, t * HEAD_DIM:(t + 1) * HEAD_DIM]

        def rows(parity, first_tile):
            return jnp.concatenate([prod_ref[first_tile + t, pl.ds(parity, MOE_ROWS, stride=2), :]
                                    for t in range(tiles)], axis=1)

        g = rows(0, 0) + rows(1, tiles)
        up = rows(0, 2 * tiles) + rows(1, 3 * tiles)
        hid = (_silu(g) * up).astype(BF16)
        y = jnp.dot(hid, wd_bf[...], preferred_element_type=F32)
        for j in range(y_ref.shape[1]):
            y_ref[:, j, :] = y[:, j * HEAD_DIM:(j + 1) * HEAD_DIM]

    @pl.when(jnp.logical_not(used))
    def _():
        y_ref[...] = jnp.zeros(y_ref.shape, F32)


def moe_ffn(up, tok_buf, blk_e, blk_used, w_gate, w_up, w_down, layer):
    p = tok_buf.shape[0]
    d, f = w_gate.shape[-2:]
    n_blk = p // MOE_ROWS
    slabs_in = up.shape[1]
    tok3 = tok_buf.reshape(n_blk, 1, MOE_ROWS)
    grid_spec = pltpu.PrefetchScalarGridSpec(
        num_scalar_prefetch=2,
        grid=(n_blk,),
        in_specs=[pl.BlockSpec((1, 1, MOE_ROWS), lambda b, e, u: (b, 0, 0), memory_space=pltpu.SMEM),
                  pl.BlockSpec((1, 1, MOE_ROWS), lambda b, e, u: (jnp.minimum(b + 1, n_blk - 1), 0, 0),
                               memory_space=pltpu.SMEM),
                  pl.BlockSpec(memory_space=pl.ANY),
                  pl.BlockSpec((1, 1, d, f), lambda b, e, u: (layer, e[b], 0, 0)),
                  pl.BlockSpec((1, 1, d, f), lambda b, e, u: (layer, e[b], 0, 0)),
                  pl.BlockSpec((1, 1, f, d), lambda b, e, u: (layer, e[b], 0, 0))],
        out_specs=pl.BlockSpec((MOE_ROWS, d // HEAD_DIM, HEAD_DIM), lambda b, e, u: (b, 0, 0)),
        scratch_shapes=[pltpu.VMEM((2, MOE_ROWS, slabs_in, HEAD_DIM), jnp.uint32), pltpu.SemaphoreType.DMA((2,)),
                        pltpu.VMEM((d // 2, 4 * f), BF16), pltpu.VMEM((f, d), BF16),
                        pltpu.VMEM((4 * f // HEAD_DIM, 2 * MOE_ROWS, HEAD_DIM), F32)],
    )
    return pl.pallas_call(
        _moe_ffn_kernel,
        grid_spec=grid_spec,
        out_shape=jax.ShapeDtypeStruct((p, d // HEAD_DIM, HEAD_DIM), F32),
        compiler_params=_params("arbitrary"),
        name="moe_ffn",
    )(blk_e, blk_used, tok3, tok3, up, w_gate, w_up, w_down)


def _combine_kernel(pos_cur, pos_next, h_ref, gw_ref, mod_ref, lnw_ref, lnb_ref, ys_hbm, *rest,
                    gate_row, shift_row, scale_row, alpha):
    out_refs, (buf, sem) = rest[:-2], rest[-2:]
    tm = h_ref.shape[0]
    i = pl.program_id(0)
    n = pl.num_programs(0)
    slot = i % 2

    def row_copy(pos_ref, to_slot, k, t):
        return pltpu.make_async_copy(ys_hbm.at[pos_ref[0, 0, k * tm + t]], buf.at[to_slot, k, t], sem.at[to_slot])

    def for_all_rows(fn):
        for k in range(MOE_TOP_K):
            _for_rows(tm, functools.partial(fn, k))

    @pl.when(i == 0)
    def _():
        for_all_rows(lambda k, t: row_copy(pos_cur, 0, k, t).start())

    @pl.when(i + 1 < n)
    def _():
        for_all_rows(lambda k, t: row_copy(pos_next, 1 - slot, k, t).start())

    for_all_rows(lambda k, t: row_copy(pos_cur, slot, k, t).wait())

    pieces = []
    for j in range(buf.shape[3]):
        piece = gw_ref[:, 0:1] * buf[slot, 0, :, j, :]
        for k in range(1, MOE_TOP_K):
            piece = piece + gw_ref[:, k:k + 1] * buf[slot, k, :, j, :]
        pieces.append(piece)
    y = jnp.concatenate(pieces, axis=1)
    gate = mod_ref[gate_row:gate_row + 1, :]
    hn = _residual_norm(h_ref[...], y, gate, lnw_ref[...], lnb_ref[...], alpha)
    out_refs[0][...] = hn
    if shift_row is not None:
        shift = mod_ref[shift_row:shift_row + 1, :]
        scale = mod_ref[scale_row:scale_row + 1, :]
        out_refs[1][...] = (hn * (1.0 + scale) + shift).astype(BF16)


def moe_combine_norm(h, ys, pos, gate_w, mod, lnw, lnb, gate_row, shift_row, scale_row, alpha, tm=128):
    s, d = h.shape
    tm = min(tm, s)
    n = s // tm
    pos_tiles = pos.reshape(n, tm, MOE_TOP_K).transpose(0, 2, 1).reshape(n, 1, MOE_TOP_K * tm)
    row = pl.BlockSpec((tm, d), lambda i: (i, 0))
    vec = pl.BlockSpec((1, d), lambda i: (0, 0))
    with_u = shift_row is not None
    out_shape = [jax.ShapeDtypeStruct((s, d), F32)] + ([jax.ShapeDtypeStruct((s, d), BF16)] if with_u else [])
    outs = pl.pallas_call(
        functools.partial(_combine_kernel, gate_row=gate_row, shift_row=shift_row, scale_row=scale_row, alpha=alpha),
        grid=(n,),
        in_specs=[pl.BlockSpec((1, 1, MOE_TOP_K * tm), lambda i: (i, 0, 0), memory_space=pltpu.SMEM),
                  pl.BlockSpec((1, 1, MOE_TOP_K * tm), lambda i: (jnp.minimum(i + 1, n - 1), 0, 0),
                               memory_space=pltpu.SMEM),
                  row,
                  pl.BlockSpec((tm, MOE_TOP_K), lambda i: (i, 0)),
                  pl.BlockSpec(mod.shape, lambda i: (0, 0)), vec, vec,
                  pl.BlockSpec(memory_space=pl.ANY)],
        out_specs=[row] * len(out_shape),
        out_shape=out_shape,
        scratch_shapes=[pltpu.VMEM((2, MOE_TOP_K, tm) + ys.shape[1:], F32), pltpu.SemaphoreType.DMA((2,))],
        compiler_params=_params("arbitrary"),
        name="moe_combine_norm",
    )(pos_tiles, pos_tiles, h, gate_w, mod, lnw.reshape(1, d), lnb.reshape(1, d), ys)
    return (outs[0], outs[1]) if with_u else (outs[0], None)


def moe_sublayer(h, up, mod, lnw, lnb, w_group, w_expert, w_gate, w_up, w_down, layer,
                 cur_shift_row, cur_scale_row, gate_row, shift_row, scale_row, alpha):
    eid, gate_w, rank, counts = moe_router(h, mod, cur_shift_row, cur_scale_row, w_group, w_expert)
    tok_buf, blk_e, blk_used, pos = _dispatch_plan(eid, rank, counts)
    ys = moe_ffn(up, tok_buf, blk_e, blk_used, w_gate, w_up, w_down, layer)
    return moe_combine_norm(h, ys, pos, gate_w, mod, lnw, lnb, gate_row, shift_row, scale_row, alpha)


def kernel(x, c, ada_w, ada_b, ln_w, ln_b, gdn_w_in, gdn_conv_w, gdn_a_log, gdn_dt_bias, gdn_norm_w, gdn_w_out,
           hgrn_w_in, hgrn_lb_logits, hgrn_norm_w, hgrn_w_out, moe_w_group, moe_w_expert, moe_w_gate, moe_w_up,
           moe_w_down):
    batch, s, d = x.shape
    assert batch == 1, "one sequence per call"
    depth = ada_w.shape[0]
    alpha = (2 * depth) ** 0.25
    mod = ada_modulation(c, ada_w, ada_b)
    h = x.reshape(s, d)
    u = modulate(h, mod, 0, 1)
    for layer in range(depth):
        base = 6 * layer
        j = layer // 2
        if layer % 2 == 0:
            y = gated_deltanet_mixer(u, gdn_w_in[j], gdn_conv_w[j], gdn_a_log[j], gdn_dt_bias[j], gdn_norm_w[j],
                                     gdn_w_out[j])
        else:
            y = hgrn2_mixer(u, hgrn_w_in[j], hgrn_lb_logits, layer, hgrn_norm_w[j], hgrn_w_out[j])
        h, u = residual_norm(h, y, mod, ln_w[layer, 0], ln_b[layer, 0], base + 2, base + 3, base + 4, alpha)
        last = layer + 1 == depth
        h, u = moe_sublayer(h, u, mod, ln_w[layer, 1], ln_b[layer, 1], moe_w_group[layer], moe_w_expert[layer],
                            moe_w_gate, moe_w_up, moe_w_down, layer,
                            cur_shift_row=base + 3, cur_scale_row=base + 4, gate_row=base + 5,
                            shift_row=None if last else base + 6, scale_row=None if last else base + 7, alpha=alpha)
    return h.reshape(batch, s, d)
```

```python
import functools

import jax
import jax.numpy as jnp
from jax import lax
from jax.experimental import pallas as pl
from jax.experimental.pallas import tpu as pltpu

F32 = jnp.float32
BF16 = jnp.bfloat16

HEAD_DIM = 128
CHUNK = 64
SUB = 16
GLA_SUB = 8
CONV_WIDTH = 4
CONV_HALO = 8
NORM_EPS = 1e-6
MOE_TOP_K = 2
MOE_ROWS = 256
DMA_QUEUES = 2
V7X_VMEM_BYTES = 64 * 1024 * 1024
VMEM_LIMIT = V7X_VMEM_BYTES - 8 * 1024 * 1024


def _params(*sem, vmem=VMEM_LIMIT):
    return pltpu.CompilerParams(dimension_semantics=sem, vmem_limit_bytes=vmem)


def _sigmoid(x):
    return 1.0 / (1.0 + jnp.exp(-x))


def _silu(x):
    return x * _sigmoid(x)


def _softplus(x):
    return jnp.maximum(x, 0.0) + jnp.log1p(jnp.exp(-jnp.abs(x)))


def _bdot(a, b):
    return jnp.dot(a.astype(BF16), b.astype(BF16), preferred_element_type=F32)


def _bdot_nt(a, b):
    return lax.dot_general(a.astype(BF16), b.astype(BF16), (((1,), (1,)), ((), ())),
                           preferred_element_type=F32)


def _bdot_tn(a, b):
    return lax.dot_general(a.astype(BF16), b.astype(BF16), (((0,), (0,)), ((), ())),
                           preferred_element_type=F32)


def _ada_kernel(c_ref, w_ref, b_ref, o_ref):
    cond = _silu(c_ref[...])
    o_ref[0] = jnp.sum(cond * w_ref[0], axis=0, keepdims=True) + b_ref[0]


def ada_modulation(c, ada_w, ada_b, tn=512):
    depth, d, n = ada_w.shape
    out = pl.pallas_call(
        _ada_kernel,
        grid=(depth, n // tn),
        in_specs=[pl.BlockSpec((d, 1), lambda l, j: (0, 0)),
                  pl.BlockSpec((1, d, tn), lambda l, j: (l, 0, j)),
                  pl.BlockSpec((1, 1, tn), lambda l, j: (l, 0, j))],
        out_specs=pl.BlockSpec((1, 1, tn), lambda l, j: (l, 0, j)),
        out_shape=jax.ShapeDtypeStruct((depth, 1, n), F32),
        compiler_params=_params("parallel", "parallel"),
        name="ada_modulation",
    )(c.reshape(d, 1), ada_w, ada_b.reshape(depth, 1, n))
    return out.reshape(depth * 6, d)


def _modulate_kernel(x_ref, mod_ref, u_ref, *, shift_row, scale_row):
    shift = mod_ref[shift_row:shift_row + 1, :]
    scale = mod_ref[scale_row:scale_row + 1, :]
    u_ref[...] = (x_ref[...] * (1.0 + scale) + shift).astype(u_ref.dtype)


def modulate(x, mod, shift_row, scale_row, tm=256):
    s, d = x.shape
    return pl.pallas_call(
        functools.partial(_modulate_kernel, shift_row=shift_row, scale_row=scale_row),
        grid=(s // tm,),
        in_specs=[pl.BlockSpec((tm, d), lambda i: (i, 0)),
                  pl.BlockSpec(mod.shape, lambda i: (0, 0))],
        out_specs=pl.BlockSpec((tm, d), lambda i: (i, 0)),
        out_shape=jax.ShapeDtypeStruct((s, d), BF16),
        compiler_params=_params("parallel"),
        name="modulate",
    )(x, mod)


def _mm_kernel(x_ref, w_ref, o_ref):
    o_ref[...] = jnp.dot(x_ref[...], w_ref[...], preferred_element_type=F32).astype(o_ref.dtype)


def matmul(x, w, tm=512, tn=1024, out_dtype=F32, name="matmul"):
    m, k = x.shape
    _, n = w.shape
    tm, tn = min(tm, m), min(tn, n)
    while n % tn:
        tn -= HEAD_DIM
    assert m % tm == 0 and tn > 0, (m, n, tm, tn)
    return pl.pallas_call(
        _mm_kernel,
        grid=(n // tn, m // tm),
        in_specs=[pl.BlockSpec((tm, k), lambda j, i: (i, 0)),
                  pl.BlockSpec((k, tn), lambda j, i: (0, j))],
        out_specs=pl.BlockSpec((tm, tn), lambda j, i: (i, j)),
        out_shape=jax.ShapeDtypeStruct((m, n), out_dtype),
        compiler_params=_params("parallel", "parallel"),
        name=name,
    )(x, w)


def _residual_norm(h, y, gate, lnw, lnb, alpha):
    v = alpha * h + (1.0 + gate) * y
    mu = jnp.mean(v, axis=-1, keepdims=True)
    dv = v - mu
    var = jnp.mean(dv * dv, axis=-1, keepdims=True)
    return dv * lax.rsqrt(var + NORM_EPS) * lnw + lnb


HI_HALF_MASK = 0xFFFF0000


def _pack_bf16_halves(u):
    half = u.shape[1] // 2
    lo = pltpu.bitcast(u[:, :half].astype(BF16).astype(F32), jnp.uint32)
    hi = pltpu.bitcast(u[:, half:].astype(BF16).astype(F32), jnp.uint32)
    return (lo >> 16) | (hi & jnp.uint32(HI_HALF_MASK))


def _unpack_bf16_halves(words):
    lo = pltpu.bitcast(words << 16, F32).astype(BF16)
    hi = pltpu.bitcast(words & jnp.uint32(HI_HALF_MASK), F32).astype(BF16)
    return lo, hi


def _ln_kernel(h_ref, y_ref, mod_ref, lnw_ref, lnb_ref, h_out, up_out, *, gate_row, shift_row, scale_row, alpha):
    gate = mod_ref[gate_row:gate_row + 1, :]
    hn = _residual_norm(h_ref[...], y_ref[...], gate, lnw_ref[...], lnb_ref[...], alpha)
    h_out[...] = hn
    shift = mod_ref[shift_row:shift_row + 1, :]
    scale = mod_ref[scale_row:scale_row + 1, :]
    words = _pack_bf16_halves(hn * (1.0 + scale) + shift)
    for j in range(up_out.shape[1]):
        up_out[:, j, :] = words[:, j * HEAD_DIM:(j + 1) * HEAD_DIM]


def residual_norm(h, y, mod, lnw, lnb, gate_row, shift_row, scale_row, alpha, tm=256):
    s, d = h.shape
    row = pl.BlockSpec((tm, d), lambda i: (i, 0))
    vec = pl.BlockSpec((1, d), lambda i: (0, 0))
    slabs = d // (2 * HEAD_DIM)
    return pl.pallas_call(
        functools.partial(_ln_kernel, gate_row=gate_row, shift_row=shift_row, scale_row=scale_row, alpha=alpha),
        grid=(s // tm,),
        in_specs=[row, row, pl.BlockSpec(mod.shape, lambda i: (0, 0)), vec, vec],
        out_specs=[row, pl.BlockSpec((tm, slabs, HEAD_DIM), lambda i: (i, 0, 0))],
        out_shape=[jax.ShapeDtypeStruct((s, d), F32), jax.ShapeDtypeStruct((s, slabs, HEAD_DIM), jnp.uint32)],
        compiler_params=_params("parallel"),
        name="residual_norm",
    )(h, y, mod, lnw.reshape(1, d), lnb.reshape(1, d))


def _gdn_gate_kernel(ab_ref, alog_ref, dtb_ref, gcum_ref, beta_ref):
    hv = alog_ref.shape[-1]
    a = ab_ref[:, :hv]
    b = ab_ref[:, hv:]
    g = -jnp.exp(alog_ref[...]) * _softplus(a + dtb_ref[...])
    beta_ref[...] = _sigmoid(b)
    tm = g.shape[0]
    row = lax.broadcasted_iota(jnp.int32, (CHUNK, CHUNK), 0)
    col = lax.broadcasted_iota(jnp.int32, (CHUNK, CHUNK), 1)
    tri = (row >= col).astype(F32)
    for c in range(tm // CHUNK):
        gcum_ref[c * CHUNK:(c + 1) * CHUNK, :] = jnp.dot(
            tri, g[c * CHUNK:(c + 1) * CHUNK, :], preferred_element_type=F32, precision=lax.Precision.HIGHEST)


def gdn_gates(ab, a_log, dt_bias, tm=512):
    s, two_hv = ab.shape
    hv = two_hv // 2
    tm = min(tm, s)
    vec = pl.BlockSpec((1, hv), lambda i: (0, 0))
    out = pl.BlockSpec((tm, hv), lambda i: (i, 0))
    return pl.pallas_call(
        _gdn_gate_kernel,
        grid=(s // tm,),
        in_specs=[pl.BlockSpec((tm, two_hv), lambda i: (i, 0)), vec, vec],
        out_specs=[out, out],
        out_shape=[jax.ShapeDtypeStruct((s, hv), F32)] * 2,
        compiler_params=_params("parallel"),
        name="gdn_gates",
    )(ab, a_log.reshape(1, hv), dt_bias.reshape(1, hv))


def _l2norm(x):
    return x * lax.rsqrt(jnp.sum(x * x, axis=-1, keepdims=True) + NORM_EPS)


PROJ_SUB = 256


def _proj_act_kernel(x_ref, w_ref, *rest, mode):
    tn = w_ref.shape[1]
    n_sub = tn // PROJ_SUB
    if mode == "silu":
        (o_ref,) = rest
        cbufs = ()
    elif mode == "conv":
        cw_ref, o_ref, *cbufs = rest
    else:
        cw_ref, sc_ref, o_ref, *cbufs = rest
    tm = o_ref.shape[0]
    x = x_ref[...]

    if mode != "silu":
        @pl.when(pl.program_id(1) == 0)
        def _():
            for cbuf in cbufs:
                cbuf[0:CONV_HALO, :] = jnp.zeros((CONV_HALO, PROJ_SUB), F32)

    def matmul_sub(c):
        cols = slice(c * PROJ_SUB, (c + 1) * PROJ_SUB)
        acc = jnp.dot(x, w_ref[:, cols], preferred_element_type=F32)
        if mode == "silu":
            return acc
        cbufs[c][CONV_HALO:CONV_HALO + tm, :] = acc
        return None

    def epilogue(c, acc):
        cols = slice(c * PROJ_SUB, (c + 1) * PROJ_SUB)
        if mode == "silu":
            o_ref[:, cols] = _silu(acc)
            return
        cbuf = cbufs[c]
        conv = None
        for j in range(CONV_WIDTH):
            start = CONV_HALO - (CONV_WIDTH - 1) + j
            term = cw_ref[j:j + 1, cols] * cbuf[start:start + tm, :]
            conv = term if conv is None else conv + term
        cbuf[0:CONV_HALO, :] = cbuf[tm:tm + CONV_HALO, :]
        act = _silu(conv)
        if mode == "conv":
            o_ref[:, cols] = act
            return
        for h in range(PROJ_SUB // HEAD_DIM):
            lanes = slice(c * PROJ_SUB + h * HEAD_DIM, c * PROJ_SUB + (h + 1) * HEAD_DIM)
            o_ref[:, lanes] = _l2norm(act[:, h * HEAD_DIM:(h + 1) * HEAD_DIM]) * sc_ref[:, lanes]

    pending = None
    for c in range(n_sub):
        if pending is not None:
            epilogue(*pending)
        pending = (c, matmul_sub(c))
    epilogue(*pending)


def proj_act(x, w, mode, conv_w=None, col_scale=None, tm=512, tn=1024, name="proj_act"):
    m, k = x.shape
    n = w.shape[1]
    tm, tn = min(tm, m), min(tn, n)
    assert m % tm == 0 and n % tn == 0 and tn % PROJ_SUB == 0
    col = pl.BlockSpec((1, tn), lambda j, i: (0, j))
    in_specs = [pl.BlockSpec((tm, k), lambda j, i: (i, 0)), pl.BlockSpec((k, tn), lambda j, i: (0, j))]
    args = [x, w]
    scratch = []
    if mode != "silu":
        in_specs.append(pl.BlockSpec((CONV_WIDTH, tn), lambda j, i: (0, j)))
        args.append(conv_w)
        scratch += [pltpu.VMEM((tm + CONV_HALO, PROJ_SUB), F32)] * (tn // PROJ_SUB)
    if mode == "conv_norm":
        in_specs.append(col)
        args.append(col_scale.reshape(1, n))
    return pl.pallas_call(
        functools.partial(_proj_act_kernel, mode=mode),
        grid=(n // tn, m // tm),
        in_specs=in_specs,
        out_specs=pl.BlockSpec((tm, tn), lambda j, i: (i, j)),
        out_shape=jax.ShapeDtypeStruct((m, n), F32),
        scratch_shapes=scratch,
        compiler_params=_params("arbitrary", "arbitrary"),
        name=name,
    )(*args)


def _unit_lower_inverse_minus_identity(mats):
    n = range(len(mats))
    row = lax.broadcasted_iota(jnp.int32, mats[0].shape, 0)
    col = lax.broadcasted_iota(jnp.int32, mats[0].shape, 1)
    same_sub = (row // SUB) == (col // SUB)
    b1 = [jnp.where(same_sub, -a, 0.0) for a in mats]
    bp = _products(b1, b1)
    b1bp = _products(b1, bp)
    e = [b1[i] + bp[i] + b1bp[i] for i in n]
    width = 4
    while width < SUB:
        bp = _products(bp, bp)
        ebp = _products(e, bp)
        e = [e[i] + bp[i] + ebp[i] for i in n]
        width *= 2
    size = SUB
    while size < CHUNK:
        lower_pair = ((row // (2 * size)) == (col // (2 * size))) & ((row // size) > (col // size))
        f = [jnp.where(lower_pair, a, 0.0) for a in mats]
        ef = _products(e, f)
        g = [f[i] + ef[i] for i in n]
        ge = _products(g, e)
        e = [e[i] - (g[i] + ge[i]) for i in n]
        size *= 2
    return e


def _products(xs, ys):
    return [_bdot(x, y) for x, y in zip(xs, ys)]


def _gdn_kernel(q_ref, k_ref, v_ref, z_ref, gcol_ref, bcol_ref, grow_ref, nw_ref, o_ref, state_ref):
    ts = q_ref.shape[0]

    @pl.when(pl.program_id(1) == 0)
    def _():
        state_ref[...] = jnp.zeros(state_ref.shape, F32)

    v_all = v_ref[...]
    nh = q_ref.shape[1] // HEAD_DIM
    rep = v_all.shape[1] // q_ref.shape[1]

    row = lax.broadcasted_iota(jnp.int32, (ts, ts), 0)
    col = lax.broadcasted_iota(jnp.int32, (ts, ts), 1)
    same_chunk = (row // CHUNK) == (col // CHUNK)
    causal = same_chunk & (row >= col)
    strict = same_chunk & (row > col)
    nw = nw_ref[...]

    heads = range(nh)
    chains = range(nh * rep)
    head_of = [c // rep for c in chains]
    lanes = [slice(c * HEAD_DIM, (c + 1) * HEAD_DIM) for c in chains]
    q = [q_ref[:, h * HEAD_DIM:(h + 1) * HEAD_DIM] for h in heads]
    k = [k_ref[:, h * HEAD_DIM:(h + 1) * HEAD_DIM] for h in heads]
    kk = [_bdot_nt(k[h], k[h]) for h in heads]
    qk_raw = [_bdot_nt(q[h], k[h]) for h in heads]
    gcol = [gcol_ref[c // rep, :, c % rep:c % rep + 1] for c in chains]
    bcol = [bcol_ref[c // rep, :, c % rep:c % rep + 1] for c in chains]
    grow = [grow_ref[c // rep, c % rep:c % rep + 1, :] for c in chains]
    glast = [_block_rows(gcol[c], CHUNK, CHUNK - 1) for c in chains]
    decay = [jnp.exp(jnp.where(causal, gcol[c] - grow[c], -jnp.inf)) for c in chains]
    e = _unit_lower_inverse_minus_identity(
        [jnp.where(strict, kk[head_of[c]] * bcol[c] * decay[c], 0.0) for c in chains])
    eg = [jnp.exp(gcol[c]) for c in chains]
    rhs = [jnp.concatenate([v_all[:, lanes[c]] * bcol[c], k[head_of[c]] * (bcol[c] * eg[c])], axis=1)
           for c in chains]
    sol = [rhs[c] + _bdot(e[c], rhs[c]) for c in chains]
    qk = [jnp.where(causal, qk_raw[head_of[c]] * decay[c], 0.0) for c in chains]
    qk_sol = [_bdot(qk[c], sol[c]) for c in chains]
    q_eff = [q[head_of[c]] * eg[c] - qk_sol[c][:, HEAD_DIM:] for c in chains]
    k_dec = [k[head_of[c]] * jnp.exp(glast[c] - gcol[c]) for c in chains]
    g_chunk = [jnp.exp(glast[c]) for c in chains]
    state = [state_ref[c] for c in chains]
    for j in range(ts // CHUNK):
        rows = slice(j * CHUNK, (j + 1) * CHUNK)
        kt_sol = [_bdot_tn(k_dec[c][rows], sol[c][rows]) for c in chains]
        prod = [_bdot(jnp.concatenate([kt_sol[c][:, HEAD_DIM:], q_eff[c][rows]], axis=0), state[c])
                for c in chains]
        state = [state[c] * g_chunk[c][j * CHUNK:j * CHUNK + 1] - prod[c][:HEAD_DIM] + kt_sol[c][:, :HEAD_DIM]
                 for c in chains]
        for c in chains:
            o = prod[c][HEAD_DIM:] + qk_sol[c][rows, :HEAD_DIM]
            o = o * lax.rsqrt(jnp.mean(o * o, axis=-1, keepdims=True) + NORM_EPS) * nw
            o_ref[rows, lanes[c]] = (o * z_ref[rows, lanes[c]]).astype(o_ref.dtype)
    for c in chains:
        state_ref[c] = state[c]


GDN_TILE = 128
GDN_HEADS_PER_STEP = 8


def gdn_recurrence(qk, v, z_act, gcum, beta, norm_w, ts=GDN_TILE, nh=GDN_HEADS_PER_STEP):
    s = qk.shape[0]
    key_dim = qk.shape[1] // 2
    value_dim = v.shape[1]
    hq = key_dim // HEAD_DIM
    hv = value_dim // HEAD_DIM
    rep = hv // hq
    nh = min(nh, hq)
    ts = min(ts, s)
    assert hq % nh == 0 and s % ts == 0
    qw = nh * HEAD_DIM
    vw = nh * rep * HEAD_DIM
    gcol = gcum.reshape(s, hq, rep).transpose(1, 0, 2)
    bcol = beta.reshape(s, hq, rep).transpose(1, 0, 2)
    grow = gcum.T.reshape(hq, rep, s)
    k_off = key_dim // qw
    return pl.pallas_call(
        _gdn_kernel,
        grid=(hq // nh, s // ts),
        in_specs=[pl.BlockSpec((ts, qw), lambda h, i: (i, h)),
                  pl.BlockSpec((ts, qw), lambda h, i: (i, k_off + h)),
                  pl.BlockSpec((ts, vw), lambda h, i: (i, h)),
                  pl.BlockSpec((ts, vw), lambda h, i: (i, h)),
                  pl.BlockSpec((nh, ts, rep), lambda h, i: (h, i, 0)),
                  pl.BlockSpec((nh, ts, rep), lambda h, i: (h, i, 0)),
                  pl.BlockSpec((nh, rep, ts), lambda h, i: (h, 0, i)),
                  pl.BlockSpec((1, HEAD_DIM), lambda h, i: (0, 0))],
        out_specs=pl.BlockSpec((ts, vw), lambda h, i: (i, h)),
        out_shape=jax.ShapeDtypeStruct((s, value_dim), BF16),
        scratch_shapes=[pltpu.VMEM((nh * rep, HEAD_DIM, HEAD_DIM), F32)],
        compiler_params=_params("parallel", "arbitrary"),
        name="gdn_recurrence",
    )(qk, qk, v, z_act, gcol, bcol, grow, norm_w.reshape(1, HEAD_DIM))


def gated_deltanet_mixer(u, w_in, conv_w, a_log, dt_bias, norm_w, w_out):
    hv = a_log.shape[0]
    value_dim = hv * HEAD_DIM
    key_dim = (w_in.shape[1] - 2 * value_dim - 2 * hv) // 2
    c_qk, c_v, c_z = 2 * key_dim, 2 * key_dim + value_dim, 2 * key_dim + 2 * value_dim
    q_scale = jnp.concatenate([jnp.full((key_dim,), HEAD_DIM ** -0.5, F32), jnp.ones((key_dim,), F32)])
    qk = proj_act(u, w_in[:, :c_qk].astype(BF16), "conv_norm", conv_w[:, :c_qk], q_scale, name="gdn_qk_proj")
    v = proj_act(u, w_in[:, c_qk:c_v].astype(BF16), "conv", conv_w[:, c_qk:c_v], name="gdn_v_proj")
    z_act = proj_act(u, w_in[:, c_v:c_z].astype(BF16), "silu", name="gdn_z_proj")
    ab = matmul(u, w_in[:, c_z:].astype(BF16), name="gdn_gate_proj")
    gcum, beta = gdn_gates(ab, a_log, dt_bias)
    o = gdn_recurrence(qk, v, z_act, gcum, beta, norm_w)
    return matmul(o, w_out.astype(BF16), tn=512, name="gdn_out_proj")


def _block_rows(x, sub, offset):
    n = x.shape[0] // sub
    x3 = x.reshape(n, sub, x.shape[1])
    return jnp.broadcast_to(x3[:, offset:offset + 1, :], x3.shape).reshape(x.shape)


def _gla_intra_chunk(q, k, b):
    row = lax.broadcasted_iota(jnp.int32, (CHUNK, CHUNK), 0)
    col = lax.broadcasted_iota(jnp.int32, (CHUNK, CHUNK), 1)
    att = jnp.zeros((CHUNK, CHUNK), F32)
    size = CHUNK // 2
    while size >= GLA_SUB:
        b_row_ref = _block_rows(b, size, 0)
        nxt = jnp.concatenate([b_row_ref[size:], jnp.broadcast_to(b[CHUNK - 1:CHUNK], (size, b.shape[1]))], axis=0)
        qs = q * jnp.exp(b - b_row_ref)
        ks = k * jnp.exp(nxt - b)
        pair = ((row // size) % 2 == 1) & ((row // size) == (col // size) + 1)
        att = att + jnp.where(pair, _bdot_nt(qs, ks), 0.0)
        size //= 2
    row_in = lax.broadcasted_iota(jnp.int32, (CHUNK, 1), 0) % GLA_SUB
    same_sub = (row // GLA_SUB) == (col // GLA_SUB)
    for j in range(GLA_SUB):
        bj = _block_rows(b, GLA_SUB, j)
        kj = _block_rows(k, GLA_SUB, j)
        e = jnp.exp(jnp.where(row_in >= j, b - bj, -jnp.inf))
        s = jnp.sum(q * e * kj, axis=-1, keepdims=True)
        att = jnp.where(same_sub & ((col % GLA_SUB) == j), s, att)
    return att


def _hgrn_lower_bound(lbl_ref, layer):
    depth = lbl_ref.shape[0]
    rows = [lbl_ref[i:i + 1, :] for i in range(depth)]
    m = functools.reduce(jnp.maximum, rows)
    ex = [jnp.exp(r - m) for r in rows]
    total = functools.reduce(lambda x, y: x + y, ex)
    acc = jnp.zeros_like(m)
    for i in range(1, layer + 1):
        acc = acc + ex[i] / total
    return acc


def _hgrn_kernel(q_ref, f_ref, i_ref, og_ref, lbl_ref, nw_ref, o_ref, state_ref, *, layer):
    ts = q_ref.shape[0]

    @pl.when(pl.program_id(1) == 0)
    def _():
        state_ref[...] = jnp.zeros(state_ref.shape, F32)

    lb = _hgrn_lower_bound(lbl_ref, layer)
    f = f_ref[...]
    log_sig = -_softplus(-f)
    t0 = jnp.log(lb)
    t1 = jnp.log1p(-lb) + log_sig
    log_f = jnp.maximum(t0, t1) + jnp.log1p(jnp.exp(-jnp.abs(t0 - t1)))
    q = _silu(q_ref[...])
    k = (1.0 - lb) * _sigmoid(-f)
    row = lax.broadcasted_iota(jnp.int32, (CHUNK, CHUNK), 0)
    col = lax.broadcasted_iota(jnp.int32, (CHUNK, CHUNK), 1)
    tri = (row >= col).astype(F32)
    nw = nw_ref[...]
    nh = q.shape[1] // HEAD_DIM
    heads = range(nh)
    units = [(h, j) for j in range(ts // CHUNK) for h in heads]
    view = {(h, j): (slice(j * CHUNK, (j + 1) * CHUNK), slice(h * HEAD_DIM, (h + 1) * HEAD_DIM)) for h, j in units}
    b = {u: jnp.dot(tri, log_f[view[u]], preferred_element_type=F32, precision=lax.Precision.HIGHEST)
         for u in units}
    att = {u: _gla_intra_chunk(q[view[u]], k[view[u]], b[u]) for u in units}
    o_intra = {u: _bdot(att[u], i_ref[view[u]]) for u in units}
    q_dec = {u: q[view[u]] * jnp.exp(b[u]) for u in units}
    k_dec = {u: k[view[u]] * jnp.exp(b[u][CHUNK - 1:CHUNK, :] - b[u]) for u in units}
    kv = {u: _bdot_tn(i_ref[view[u]], k_dec[u]) for u in units}
    state_t = [state_ref[h] for h in heads]
    for j in range(ts // CHUNK):
        o = [o_intra[h, j] + _bdot_nt(q_dec[h, j], state_t[h]) for h in heads]
        state_t = [state_t[h] * jnp.exp(b[h, j][CHUNK - 1:CHUNK, :]) + kv[h, j] for h in heads]
        for h in heads:
            oh = o[h] * lax.rsqrt(jnp.mean(o[h] * o[h], axis=-1, keepdims=True) + NORM_EPS) * nw
            o_ref[view[h, j]] = (oh * _silu(og_ref[view[h, j]])).astype(o_ref.dtype)
    for h in heads:
        state_ref[h] = state_t[h]


HGRN_TILE = 256
HGRN_HEADS_PER_STEP = 4


def hgrn_recurrence(proj, lb_logits, layer, norm_w, ts=HGRN_TILE, nh=HGRN_HEADS_PER_STEP):
    s, n = proj.shape
    d = n // 4
    heads = d // HEAD_DIM
    ts = min(ts, s)
    nh = min(nh, heads)
    assert heads % nh == 0 and s % ts == 0
    depth = lb_logits.shape[0]
    groups = heads // nh
    width = nh * HEAD_DIM

    def col(off):
        return pl.BlockSpec((ts, width), lambda h, i: (i, off * groups + h))

    return pl.pallas_call(
        functools.partial(_hgrn_kernel, layer=layer),
        grid=(groups, s // ts),
        in_specs=[col(0), col(1), col(2), col(3),
                  pl.BlockSpec((depth, width), lambda h, i: (0, h)),
                  pl.BlockSpec((1, HEAD_DIM), lambda h, i: (0, 0))],
        out_specs=pl.BlockSpec((ts, width), lambda h, i: (i, h)),
        out_shape=jax.ShapeDtypeStruct((s, d), BF16),
        scratch_shapes=[pltpu.VMEM((nh, HEAD_DIM, HEAD_DIM), F32)],
        compiler_params=_params("parallel", "arbitrary"),
        name="hgrn_recurrence",
    )(proj, proj, proj, proj, lb_logits, norm_w.reshape(1, HEAD_DIM))


def hgrn2_mixer(u, w_in, lb_logits, layer, norm_w, w_out):
    proj = matmul(u, w_in.astype(BF16), name="hgrn_in_proj")
    o = hgrn_recurrence(proj, lb_logits, layer, norm_w)
    return matmul(o, w_out.astype(BF16), name="hgrn_out_proj")


ROUTER_LANES = 128
LANE_SENTINEL = 1 << 20


def _router_kernel(h_ref, mod_ref, w_ref, eid_ref, gate_ref, rank_ref, count_ref, run_ref, *,
                   shift_row, scale_row, groups, epg_shift):
    @pl.when(pl.program_id(0) == 0)
    def _():
        run_ref[...] = jnp.zeros(run_ref.shape, F32)

    shift = mod_ref[shift_row:shift_row + 1, :]
    scale = mod_ref[scale_row:scale_row + 1, :]
    u = h_ref[...] * (1.0 + scale) + shift
    logits = jnp.dot(u, w_ref[...], preferred_element_type=F32, precision=lax.Precision.HIGHEST)
    lane = lax.broadcasted_iota(jnp.int32, logits.shape, 1)
    n_exp = groups << epg_shift

    def first_argmax(vals):
        m = jnp.max(vals, axis=-1, keepdims=True)
        idx = jnp.min(jnp.where(vals == m, lane, LANE_SENTINEL), axis=-1, keepdims=True)
        return m, idx

    is_group = lane < groups
    gmax, grp = first_argmax(jnp.where(is_group, logits, -jnp.inf))
    p_grp = 1.0 / jnp.sum(jnp.where(is_group, jnp.exp(logits - gmax), 0.0), axis=-1, keepdims=True)
    e_lane = lane - groups
    in_grp = (e_lane >= 0) & (e_lane < n_exp) & (lax.shift_right_arithmetic(e_lane, epg_shift) == grp)
    el = jnp.where(in_grp, logits, -jnp.inf)
    m1, i1 = first_argmax(el)
    z = jnp.sum(jnp.where(in_grp, jnp.exp(logits - m1), 0.0), axis=-1, keepdims=True)
    m2, i2 = first_argmax(jnp.where(lane == i1, -jnp.inf, el))
    p1 = 1.0 / z
    p2 = jnp.exp(m2 - m1) / z
    denom = p1 + p2
    e1, e2 = i1 - groups, i2 - groups
    eid_ref[...] = jnp.where(lane == 0, e1, jnp.where(lane == 1, e2, 0))
    gate_ref[...] = jnp.where(lane == 0, p_grp * p1 / denom, jnp.where(lane == 1, p_grp * p2 / denom, 0.0))

    tm = logits.shape[0]
    tri = (lax.broadcasted_iota(jnp.int32, (tm, tm), 0) >= lax.broadcasted_iota(jnp.int32, (tm, tm), 1)).astype(BF16)
    hot1 = (lane == e1).astype(F32)
    hot2 = (lane == e2).astype(F32)
    cum1 = jnp.dot(tri, hot1.astype(BF16), preferred_element_type=F32)
    cum2 = jnp.dot(tri, hot2.astype(BF16), preferred_element_type=F32)
    run = run_ref[...]
    after1 = run + cum1[tm - 1:tm, :]
    rank1 = jnp.sum(hot1 * (run + cum1), axis=-1, keepdims=True) - 1.0
    rank2 = jnp.sum(hot2 * (after1 + cum2), axis=-1, keepdims=True) - 1.0
    total = after1 + cum2[tm - 1:tm, :]
    run_ref[...] = total
    count_ref[...] = total.astype(jnp.int32)
    rank_ref[...] = jnp.where(lane == 0, rank1, jnp.where(lane == 1, rank2, 0.0)).astype(jnp.int32)


def moe_router(h, mod, shift_row, scale_row, w_group, w_expert, tm=256):
    s, d = h.shape
    groups = w_group.shape[1]
    n_exp = w_expert.shape[1]
    epg = n_exp // groups
    assert epg & (epg - 1) == 0 and groups + n_exp <= ROUTER_LANES
    w = jnp.concatenate([w_group, w_expert, jnp.zeros((d, ROUTER_LANES - groups - n_exp), F32)], axis=1)
    out = pl.BlockSpec((tm, ROUTER_LANES), lambda i: (i, 0))
    ints = jax.ShapeDtypeStruct((s, ROUTER_LANES), jnp.int32)
    eid, gate, rank, counts = pl.pallas_call(
        functools.partial(_router_kernel, shift_row=shift_row, scale_row=scale_row, groups=groups,
                          epg_shift=epg.bit_length() - 1),
        grid=(s // tm,),
        in_specs=[pl.BlockSpec((tm, d), lambda i: (i, 0)),
                  pl.BlockSpec(mod.shape, lambda i: (0, 0)),
                  pl.BlockSpec((d, ROUTER_LANES), lambda i: (0, 0))],
        out_specs=[out, out, out, pl.BlockSpec((1, ROUTER_LANES), lambda i: (0, 0))],
        out_shape=[ints, jax.ShapeDtypeStruct((s, ROUTER_LANES), F32), ints,
                   jax.ShapeDtypeStruct((1, ROUTER_LANES), jnp.int32)],
        scratch_shapes=[pltpu.VMEM((1, ROUTER_LANES), F32)],
        compiler_params=_params("arbitrary"),
        name="moe_router",
    )(h, mod, w)
    return eid[:, :MOE_TOP_K], gate[:, :MOE_TOP_K], rank[:, :MOE_TOP_K], counts[0, :n_exp]


def _dispatch_plan(eid, rank, counts):
    s = eid.shape[0]
    n_exp = counts.shape[0]
    a = s * MOE_TOP_K
    padded = (counts + MOE_ROWS - 1) // MOE_ROWS * MOE_ROWS
    pad_end = jnp.cumsum(padded)
    pad_start = pad_end - padded
    dest = pad_start[eid] + rank
    n_blk = -(-a // MOE_ROWS) + n_exp
    tok_buf = jnp.zeros((n_blk * MOE_ROWS,), jnp.int32).at[dest.reshape(a)].set(
        jnp.arange(a, dtype=jnp.int32) // MOE_TOP_K)
    blk_start = jnp.arange(n_blk, dtype=jnp.int32) * MOE_ROWS
    blk_e = jnp.minimum(jnp.searchsorted(pad_end, blk_start, side="right"), n_exp - 1).astype(jnp.int32)
    blk_used = (blk_start < pad_end[-1]).astype(jnp.int32)
    return tok_buf, blk_e, blk_used, dest


def _for_rows(n, fn):
    for r in range(n):
        fn(r)


def _moe_ffn_kernel(blk_e_ref, blk_used_ref, tok_cur, tok_next, up_hbm, wg_ref, wu_ref, wd_ref, y_ref,
                    xbuf, sem, w_in_bf, wd_bf, prod_ref):
    f, d = wd_bf.shape
    half = d // 2
    b = pl.program_id(0)
    n = pl.num_programs(0)
    slot = b % 2
    used = blk_used_ref[b] > 0
    next_used = blk_used_ref[jnp.minimum(b + 1, n - 1)] > 0

    def row_copy(tok_ref, to_slot, r):
        return pltpu.make_async_copy(up_hbm.at[tok_ref[0, 0, r]], xbuf.at[to_slot, r], sem.at[to_slot])

    @pl.when((b == 0) & used)
    def _():
        _for_rows(MOE_ROWS, lambda r: row_copy(tok_cur, 0, r).start(priority=r % DMA_QUEUES))

    @pl.when((b + 1 < n) & next_used)
    def _():
        _for_rows(MOE_ROWS, lambda r: row_copy(tok_next, 1 - slot, r).start(priority=r % DMA_QUEUES))

    @pl.when(used & ((b == 0) | (blk_e_ref[b] != blk_e_ref[jnp.maximum(b - 1, 0)])))
    def _():
        w_in_bf[:, 0 * f:1 * f] = wg_ref[0, 0, 0:half, :].astype(BF16)
        w_in_bf[:, 1 * f:2 * f] = wg_ref[0, 0, half:d, :].astype(BF16)
        w_in_bf[:, 2 * f:3 * f] = wu_ref[0, 0, 0:half, :].astype(BF16)
        w_in_bf[:, 3 * f:4 * f] = wu_ref[0, 0, half:d, :].astype(BF16)
        wd_bf[...] = wd_ref[0, 0].astype(BF16)

    @pl.when(used)
    def _():
        _for_rows(MOE_ROWS, lambda r: row_copy(tok_cur, slot, r).wait())
        x2 = jnp.concatenate([pltpu.bitcast(xbuf[slot, :, j, :], BF16) for j in range(xbuf.shape[2])], axis=1)
        prod = jnp.dot(x2, w_in_bf[...], preferred_element_type=F32)
        tiles = f // HEAD_DIM
        for t in range(4 * tiles):
            prod_ref[t] = prod[:, t * HEAD_DIM:(t + 1) * HEAD_DIM]

        def rows(parity, first_tile):
            return jnp.concatenate([prod_ref[first_tile + t, pl.ds(parity, MOE_ROWS, stride=2), :]
                                    for t in range(tiles)], axis=1)

        g = rows(0, 0) + rows(1, tiles)
        up = rows(0, 2 * tiles) + rows(1, 3 * tiles)
        hid = (_silu(g) * up).astype(BF16)
        y = jnp.dot(hid, wd_bf[...], preferred_element_type=F32)
        for j in range(y_ref.shape[1]):
            y_ref[:, j, :] = y[:, j * HEAD_DIM:(j + 1) * HEAD_DIM]

    @pl.when(jnp.logical_not(used))
    def _():
        y_ref[...] = jnp.zeros(y_ref.shape, F32)


def moe_ffn(up, tok_buf, blk_e, blk_used, w_gate, w_up, w_down, layer):
    p = tok_buf.shape[0]
    d, f = w_gate.shape[-2:]
    n_blk = p // MOE_ROWS
    slabs_in = up.shape[1]
    tok3 = tok_buf.reshape(n_blk, 1, MOE_ROWS)
    grid_spec = pltpu.PrefetchScalarGridSpec(
        num_scalar_prefetch=2,
        grid=(n_blk,),
        in_specs=[pl.BlockSpec((1, 1, MOE_ROWS), lambda b, e, u: (b, 0, 0), memory_space=pltpu.SMEM),
                  pl.BlockSpec((1, 1, MOE_ROWS), lambda b, e, u: (jnp.minimum(b + 1, n_blk - 1), 0, 0),
                               memory_space=pltpu.SMEM),
                  pl.BlockSpec(memory_space=pl.ANY),
                  pl.BlockSpec((1, 1, d, f), lambda b, e, u: (layer, e[b], 0, 0)),
                  pl.BlockSpec((1, 1, d, f), lambda b, e, u: (layer, e[b], 0, 0)),
                  pl.BlockSpec((1, 1, f, d), lambda b, e, u: (layer, e[b], 0, 0))],
        out_specs=pl.BlockSpec((MOE_ROWS, d // HEAD_DIM, HEAD_DIM), lambda b, e, u: (b, 0, 0)),
        scratch_shapes=[pltpu.VMEM((2, MOE_ROWS, slabs_in, HEAD_DIM), jnp.uint32), pltpu.SemaphoreType.DMA((2,)),
                        pltpu.VMEM((d // 2, 4 * f), BF16), pltpu.VMEM((f, d), BF16),
                        pltpu.VMEM((4 * f // HEAD_DIM, 2 * MOE_ROWS, HEAD_DIM), F32)],
    )
    return pl.pallas_call(
        _moe_ffn_kernel,
        grid_spec=grid_spec,
        out_shape=jax.ShapeDtypeStruct((p, d // HEAD_DIM, HEAD_DIM), F32),
        compiler_params=_params("arbitrary"),
        name="moe_ffn",
    )(blk_e, blk_used, tok3, tok3, up, w_gate, w_up, w_down)


def _combine_kernel(pos_cur, pos_next, h_ref, gw_ref, mod_ref, lnw_ref, lnb_ref, ys_hbm, *rest,
                    gate_row, shift_row, scale_row, alpha):
    out_refs, (buf, sem) = rest[:-2], rest[-2:]
    tm = h_ref.shape[0]
    i = pl.program_id(0)
    n = pl.num_programs(0)
    slot = i % 2

    def row_copy(pos_ref, to_slot, k, t):
        return pltpu.make_async_copy(ys_hbm.at[pos_ref[0, 0, k * tm + t]], buf.at[to_slot, k, t], sem.at[to_slot])

    def for_all_rows(fn):
        for k in range(MOE_TOP_K):
            _for_rows(tm, functools.partial(fn, k))

    @pl.when(i == 0)
    def _():
        for_all_rows(lambda k, t: row_copy(pos_cur, 0, k, t).start(priority=t % DMA_QUEUES))

    @pl.when(i + 1 < n)
    def _():
        for_all_rows(lambda k, t: row_copy(pos_next, 1 - slot, k, t).start(priority=t % DMA_QUEUES))

    for_all_rows(lambda k, t: row_copy(pos_cur, slot, k, t).wait())

    pieces = []
    for j in range(buf.shape[3]):
        piece = gw_ref[:, 0:1] * buf[slot, 0, :, j, :]
        for k in range(1, MOE_TOP_K):
            piece = piece + gw_ref[:, k:k + 1] * buf[slot, k, :, j, :]
        pieces.append(piece)
    y = jnp.concatenate(pieces, axis=1)
    gate = mod_ref[gate_row:gate_row + 1, :]
    hn = _residual_norm(h_ref[...], y, gate, lnw_ref[...], lnb_ref[...], alpha)
    out_refs[0][...] = hn
    if shift_row is not None:
        shift = mod_ref[shift_row:shift_row + 1, :]
        scale = mod_ref[scale_row:scale_row + 1, :]
        out_refs[1][...] = (hn * (1.0 + scale) + shift).astype(BF16)


def moe_combine_norm(h, ys, pos, gate_w, mod, lnw, lnb, gate_row, shift_row, scale_row, alpha, tm=128):
    s, d = h.shape
    tm = min(tm, s)
    n = s // tm
    pos_tiles = pos.reshape(n, tm, MOE_TOP_K).transpose(0, 2, 1).reshape(n, 1, MOE_TOP_K * tm)
    row = pl.BlockSpec((tm, d), lambda i: (i, 0))
    vec = pl.BlockSpec((1, d), lambda i: (0, 0))
    with_u = shift_row is not None
    out_shape = [jax.ShapeDtypeStruct((s, d), F32)] + ([jax.ShapeDtypeStruct((s, d), BF16)] if with_u else [])
    outs = pl.pallas_call(
        functools.partial(_combine_kernel, gate_row=gate_row, shift_row=shift_row, scale_row=scale_row, alpha=alpha),
        grid=(n,),
        in_specs=[pl.BlockSpec((1, 1, MOE_TOP_K * tm), lambda i: (i, 0, 0), memory_space=pltpu.SMEM),
                  pl.BlockSpec((1, 1, MOE_TOP_K * tm), lambda i: (jnp.minimum(i + 1, n - 1), 0, 0),
                               memory_space=pltpu.SMEM),
                  row,
                  pl.BlockSpec((tm, MOE_TOP_K), lambda i: (i, 0)),
                  pl.BlockSpec(mod.shape, lambda i: (0, 0)), vec, vec,
                  pl.BlockSpec(memory_space=pl.ANY)],
        out_specs=[row] * len(out_shape),
        out_shape=out_shape,
        scratch_shapes=[pltpu.VMEM((2, MOE_TOP_K, tm) + ys.shape[1:], F32), pltpu.SemaphoreType.DMA((2,))],
        compiler_params=_params("arbitrary"),
        name="moe_combine_norm",
    )(pos_tiles, pos_tiles, h, gate_w, mod, lnw.reshape(1, d), lnb.reshape(1, d), ys)
    return (outs[0], outs[1]) if with_u else (outs[0], None)


def moe_sublayer(h, up, mod, lnw, lnb, w_group, w_expert, w_gate, w_up, w_down, layer,
                 cur_shift_row, cur_scale_row, gate_row, shift_row, scale_row, alpha):
    eid, gate_w, rank, counts = moe_router(h, mod, cur_shift_row, cur_scale_row, w_group, w_expert)
    tok_buf, blk_e, blk_used, pos = _dispatch_plan(eid, rank, counts)
    ys = moe_ffn(up, tok_buf, blk_e, blk_used, w_gate, w_up, w_down, layer)
    return moe_combine_norm(h, ys, pos, gate_w, mod, lnw, lnb, gate_row, shift_row, scale_row, alpha)


def kernel(x, c, ada_w, ada_b, ln_w, ln_b, gdn_w_in, gdn_conv_w, gdn_a_log, gdn_dt_bias, gdn_norm_w, gdn_w_out,
           hgrn_w_in, hgrn_lb_logits, hgrn_norm_w, hgrn_w_out, moe_w_group, moe_w_expert, moe_w_gate, moe_w_up,
           moe_w_down):
    batch, s, d = x.shape
    assert batch == 1, "one sequence per call"
    depth = ada_w.shape[0]
    alpha = (2 * depth) ** 0.25
    mod = ada_modulation(c, ada_w, ada_b)
    h = x.reshape(s, d)
    u = modulate(h, mod, 0, 1)
    for layer in range(depth):
        base = 6 * layer
        j = layer // 2
        if layer % 2 == 0:
            y = gated_deltanet_mixer(u, gdn_w_in[j], gdn_conv_w[j], gdn_a_log[j], gdn_dt_bias[j], gdn_norm_w[j],
                                     gdn_w_out[j])
        else:
            y = hgrn2_mixer(u, hgrn_w_in[j], hgrn_lb_logits, layer, hgrn_norm_w[j], hgrn_w_out[j])
        h, u = residual_norm(h, y, mod, ln_w[layer, 0], ln_b[layer, 0], base + 2, base + 3, base + 4, alpha)
        last = layer + 1 == depth
        h, u = moe_sublayer(h, u, mod, ln_w[layer, 1], ln_b[layer, 1], moe_w_group[layer], moe_w_expert[layer],
                            moe_w_gate, moe_w_up, moe_w_down, layer,
                            cur_shift_row=base + 3, cur_scale_row=base + 4, gate_row=base + 5,
                            shift_row=None if last else base + 6, scale_row=None if last else base + 7, alpha=alpha)
    return h.reshape(batch, s, d)
```

```python
import functools

import jax
import jax.numpy as jnp
from jax import lax
from jax.experimental import pallas as pl
from jax.experimental.pallas import tpu as pltpu

F32 = jnp.float32
BF16 = jnp.bfloat16

HEAD_DIM = 128
CHUNK = 64
SUB = 16
GLA_SUB = 8
CONV_WIDTH = 4
CONV_HALO = 8
NORM_EPS = 1e-6
LOG2_E = 1.4426950408889634
MOE_TOP_K = 2
MOE_ROWS = 256
DMA_QUEUES = 2
V7X_VMEM_BYTES = 64 * 1024 * 1024
VMEM_LIMIT = V7X_VMEM_BYTES - 8 * 1024 * 1024


def _params(*sem, vmem=VMEM_LIMIT):
    return pltpu.CompilerParams(dimension_semantics=sem, vmem_limit_bytes=vmem)


def _sigmoid(x):
    return 1.0 / (1.0 + jnp.exp(-x))


def _silu(x):
    return x * _sigmoid(x)


def _softplus(x):
    return jnp.maximum(x, 0.0) + jnp.log1p(jnp.exp(-jnp.abs(x)))


def _bdot(a, b):
    return jnp.dot(a.astype(BF16), b.astype(BF16), preferred_element_type=F32)


def _bdot_nt(a, b):
    return lax.dot_general(a.astype(BF16), b.astype(BF16), (((1,), (1,)), ((), ())),
                           preferred_element_type=F32)


def _bdot_tn(a, b):
    return lax.dot_general(a.astype(BF16), b.astype(BF16), (((0,), (0,)), ((), ())),
                           preferred_element_type=F32)


def _ada_kernel(c_ref, w_ref, b_ref, o_ref):
    cond = _silu(c_ref[...])
    o_ref[0] = jnp.sum(cond * w_ref[0], axis=0, keepdims=True) + b_ref[0]


def ada_modulation(c, ada_w, ada_b, tn=512):
    depth, d, n = ada_w.shape
    out = pl.pallas_call(
        _ada_kernel,
        grid=(depth, n // tn),
        in_specs=[pl.BlockSpec((d, 1), lambda l, j: (0, 0)),
                  pl.BlockSpec((1, d, tn), lambda l, j: (l, 0, j)),
                  pl.BlockSpec((1, 1, tn), lambda l, j: (l, 0, j))],
        out_specs=pl.BlockSpec((1, 1, tn), lambda l, j: (l, 0, j)),
        out_shape=jax.ShapeDtypeStruct((depth, 1, n), F32),
        compiler_params=_params("parallel", "parallel"),
        name="ada_modulation",
    )(c.reshape(d, 1), ada_w, ada_b.reshape(depth, 1, n))
    return out.reshape(depth * 6, d)


def _modulate_kernel(x_ref, mod_ref, u_ref, *, shift_row, scale_row):
    shift = mod_ref[shift_row:shift_row + 1, :]
    scale = mod_ref[scale_row:scale_row + 1, :]
    u_ref[...] = (x_ref[...] * (1.0 + scale) + shift).astype(u_ref.dtype)


def modulate(x, mod, shift_row, scale_row, tm=256):
    s, d = x.shape
    return pl.pallas_call(
        functools.partial(_modulate_kernel, shift_row=shift_row, scale_row=scale_row),
        grid=(s // tm,),
        in_specs=[pl.BlockSpec((tm, d), lambda i: (i, 0)),
                  pl.BlockSpec(mod.shape, lambda i: (0, 0))],
        out_specs=pl.BlockSpec((tm, d), lambda i: (i, 0)),
        out_shape=jax.ShapeDtypeStruct((s, d), BF16),
        compiler_params=_params("parallel"),
        name="modulate",
    )(x, mod)


def _mm_kernel(x_ref, w_ref, o_ref):
    o_ref[...] = jnp.dot(x_ref[...], w_ref[...], preferred_element_type=F32).astype(o_ref.dtype)


def matmul(x, w, tm=512, tn=1024, out_dtype=F32, name="matmul"):
    m, k = x.shape
    _, n = w.shape
    tm, tn = min(tm, m), min(tn, n)
    while n % tn:
        tn -= HEAD_DIM
    assert m % tm == 0 and tn > 0, (m, n, tm, tn)
    return pl.pallas_call(
        _mm_kernel,
        grid=(n // tn, m // tm),
        in_specs=[pl.BlockSpec((tm, k), lambda j, i: (i, 0)),
                  pl.BlockSpec((k, tn), lambda j, i: (0, j))],
        out_specs=pl.BlockSpec((tm, tn), lambda j, i: (i, j)),
        out_shape=jax.ShapeDtypeStruct((m, n), out_dtype),
        compiler_params=_params("parallel", "parallel"),
        name=name,
    )(x, w)


def _residual_norm(h, y, gate, lnw, lnb, alpha):
    v = alpha * h + (1.0 + gate) * y
    mu = jnp.mean(v, axis=-1, keepdims=True)
    dv = v - mu
    var = jnp.mean(dv * dv, axis=-1, keepdims=True)
    return dv * lax.rsqrt(var + NORM_EPS) * lnw + lnb


HI_HALF_MASK = 0xFFFF0000


def _pack_bf16_halves(u):
    half = u.shape[1] // 2
    lo = pltpu.bitcast(u[:, :half].astype(BF16).astype(F32), jnp.uint32)
    hi = pltpu.bitcast(u[:, half:].astype(BF16).astype(F32), jnp.uint32)
    return (lo >> 16) | (hi & jnp.uint32(HI_HALF_MASK))


def _unpack_bf16_halves(words):
    lo = pltpu.bitcast(words << 16, F32).astype(BF16)
    hi = pltpu.bitcast(words & jnp.uint32(HI_HALF_MASK), F32).astype(BF16)
    return lo, hi


def _ln_kernel(h_ref, y_ref, mod_ref, lnw_ref, lnb_ref, h_out, up_out, *, gate_row, shift_row, scale_row, alpha):
    gate = mod_ref[gate_row:gate_row + 1, :]
    hn = _residual_norm(h_ref[...], y_ref[...], gate, lnw_ref[...], lnb_ref[...], alpha)
    h_out[...] = hn
    shift = mod_ref[shift_row:shift_row + 1, :]
    scale = mod_ref[scale_row:scale_row + 1, :]
    words = _pack_bf16_halves(hn * (1.0 + scale) + shift)
    for j in range(up_out.shape[1]):
        up_out[:, j, :] = words[:, j * HEAD_DIM:(j + 1) * HEAD_DIM]


def residual_norm(h, y, mod, lnw, lnb, gate_row, shift_row, scale_row, alpha, tm=256):
    s, d = h.shape
    row = pl.BlockSpec((tm, d), lambda i: (i, 0))
    vec = pl.BlockSpec((1, d), lambda i: (0, 0))
    slabs = d // (2 * HEAD_DIM)
    return pl.pallas_call(
        functools.partial(_ln_kernel, gate_row=gate_row, shift_row=shift_row, scale_row=scale_row, alpha=alpha),
        grid=(s // tm,),
        in_specs=[row, row, pl.BlockSpec(mod.shape, lambda i: (0, 0)), vec, vec],
        out_specs=[row, pl.BlockSpec((tm, slabs, HEAD_DIM), lambda i: (i, 0, 0))],
        out_shape=[jax.ShapeDtypeStruct((s, d), F32), jax.ShapeDtypeStruct((s, slabs, HEAD_DIM), jnp.uint32)],
        compiler_params=_params("parallel"),
        name="residual_norm",
    )(h, y, mod, lnw.reshape(1, d), lnb.reshape(1, d))


def _gdn_gate_kernel(ab_ref, alog_ref, dtb_ref, gcum_ref, beta_ref):
    hv = alog_ref.shape[-1]
    a = ab_ref[:, :hv]
    b = ab_ref[:, hv:]
    g = -jnp.exp(alog_ref[...]) * _softplus(a + dtb_ref[...]) * LOG2_E
    beta_ref[...] = _sigmoid(b)
    tm = g.shape[0]
    row = lax.broadcasted_iota(jnp.int32, (CHUNK, CHUNK), 0)
    col = lax.broadcasted_iota(jnp.int32, (CHUNK, CHUNK), 1)
    tri = (row >= col).astype(F32)
    for c in range(tm // CHUNK):
        gcum_ref[c * CHUNK:(c + 1) * CHUNK, :] = jnp.dot(
            tri, g[c * CHUNK:(c + 1) * CHUNK, :], preferred_element_type=F32, precision=lax.Precision.HIGHEST)


def gdn_gates(ab, a_log, dt_bias, tm=512):
    s, two_hv = ab.shape
    hv = two_hv // 2
    tm = min(tm, s)
    vec = pl.BlockSpec((1, hv), lambda i: (0, 0))
    out = pl.BlockSpec((tm, hv), lambda i: (i, 0))
    return pl.pallas_call(
        _gdn_gate_kernel,
        grid=(s // tm,),
        in_specs=[pl.BlockSpec((tm, two_hv), lambda i: (i, 0)), vec, vec],
        out_specs=[out, out],
        out_shape=[jax.ShapeDtypeStruct((s, hv), F32)] * 2,
        compiler_params=_params("parallel"),
        name="gdn_gates",
    )(ab, a_log.reshape(1, hv), dt_bias.reshape(1, hv))


def _l2norm(x):
    return x * lax.rsqrt(jnp.sum(x * x, axis=-1, keepdims=True) + NORM_EPS)


PROJ_SUB = 256


def _proj_act_kernel(x_ref, w_ref, *rest, mode):
    tn = w_ref.shape[1]
    n_sub = tn // PROJ_SUB
    if mode == "silu":
        (o_ref,) = rest
        cbufs = ()
    elif mode == "conv":
        cw_ref, o_ref, *cbufs = rest
    else:
        cw_ref, sc_ref, o_ref, *cbufs = rest
    tm = o_ref.shape[0]
    x = x_ref[...]

    if mode != "silu":
        @pl.when(pl.program_id(1) == 0)
        def _():
            for cbuf in cbufs:
                cbuf[0:CONV_HALO, :] = jnp.zeros((CONV_HALO, PROJ_SUB), F32)

    def matmul_sub(c):
        cols = slice(c * PROJ_SUB, (c + 1) * PROJ_SUB)
        acc = jnp.dot(x, w_ref[:, cols], preferred_element_type=F32)
        if mode == "silu":
            return acc
        cbufs[c][CONV_HALO:CONV_HALO + tm, :] = acc
        return None

    def epilogue(c, acc):
        cols = slice(c * PROJ_SUB, (c + 1) * PROJ_SUB)
        if mode == "silu":
            o_ref[:, cols] = _silu(acc)
            return
        cbuf = cbufs[c]
        conv = None
        for j in range(CONV_WIDTH):
            start = CONV_HALO - (CONV_WIDTH - 1) + j
            term = cw_ref[j:j + 1, cols] * cbuf[start:start + tm, :]
            conv = term if conv is None else conv + term
        cbuf[0:CONV_HALO, :] = cbuf[tm:tm + CONV_HALO, :]
        act = _silu(conv)
        if mode == "conv":
            o_ref[:, cols] = act
            return
        for h in range(PROJ_SUB // HEAD_DIM):
            lanes = slice(c * PROJ_SUB + h * HEAD_DIM, c * PROJ_SUB + (h + 1) * HEAD_DIM)
            o_ref[:, lanes] = _l2norm(act[:, h * HEAD_DIM:(h + 1) * HEAD_DIM]) * sc_ref[:, lanes]

    pending = None
    for c in range(n_sub):
        if pending is not None:
            epilogue(*pending)
        pending = (c, matmul_sub(c))
    epilogue(*pending)


def proj_act(x, w, mode, conv_w=None, col_scale=None, tm=512, tn=1024, name="proj_act"):
    m, k = x.shape
    n = w.shape[1]
    tm, tn = min(tm, m), min(tn, n)
    assert m % tm == 0 and n % tn == 0 and tn % PROJ_SUB == 0
    col = pl.BlockSpec((1, tn), lambda j, i: (0, j))
    in_specs = [pl.BlockSpec((tm, k), lambda j, i: (i, 0)), pl.BlockSpec((k, tn), lambda j, i: (0, j))]
    args = [x, w]
    scratch = []
    if mode != "silu":
        in_specs.append(pl.BlockSpec((CONV_WIDTH, tn), lambda j, i: (0, j)))
        args.append(conv_w)
        scratch += [pltpu.VMEM((tm + CONV_HALO, PROJ_SUB), F32)] * (tn // PROJ_SUB)
    if mode == "conv_norm":
        in_specs.append(col)
        args.append(col_scale.reshape(1, n))
    return pl.pallas_call(
        functools.partial(_proj_act_kernel, mode=mode),
        grid=(n // tn, m // tm),
        in_specs=in_specs,
        out_specs=pl.BlockSpec((tm, tn), lambda j, i: (i, j)),
        out_shape=jax.ShapeDtypeStruct((m, n), F32),
        scratch_shapes=scratch,
        compiler_params=_params("arbitrary", "arbitrary"),
        name=name,
    )(*args)


def _unit_lower_inverse_minus_identity(mats):
    n = range(len(mats))
    row = lax.broadcasted_iota(jnp.int32, mats[0].shape, 0)
    col = lax.broadcasted_iota(jnp.int32, mats[0].shape, 1)
    same_sub = (row // SUB) == (col // SUB)
    b1 = [jnp.where(same_sub, -a, 0.0) for a in mats]
    bp = _products(b1, b1)
    b1bp = _products(b1, bp)
    e = [b1[i] + bp[i] + b1bp[i] for i in n]
    width = 4
    while width < SUB:
        bp = _products(bp, bp)
        ebp = _products(e, bp)
        e = [e[i] + bp[i] + ebp[i] for i in n]
        width *= 2
    size = SUB
    while size < CHUNK:
        lower_pair = ((row // (2 * size)) == (col // (2 * size))) & ((row // size) > (col // size))
        f = [jnp.where(lower_pair, a, 0.0) for a in mats]
        ef = _products(e, f)
        g = [f[i] + ef[i] for i in n]
        ge = _products(g, e)
        e = [e[i] - (g[i] + ge[i]) for i in n]
        size *= 2
    return e


def _products(xs, ys):
    return [_bdot(x, y) for x, y in zip(xs, ys)]


def _gdn_kernel(q_ref, k_ref, v_ref, z_ref, gcol_ref, bcol_ref, grow_ref, nw_ref, o_ref, state_ref):
    ts = q_ref.shape[0]

    @pl.when(pl.program_id(1) == 0)
    def _():
        state_ref[...] = jnp.zeros(state_ref.shape, F32)

    v_all = v_ref[...]
    nh = q_ref.shape[1] // HEAD_DIM
    rep = v_all.shape[1] // q_ref.shape[1]

    row = lax.broadcasted_iota(jnp.int32, (ts, ts), 0)
    col = lax.broadcasted_iota(jnp.int32, (ts, ts), 1)
    same_chunk = (row // CHUNK) == (col // CHUNK)
    causal = same_chunk & (row >= col)
    strict = same_chunk & (row > col)
    nw = nw_ref[...]

    heads = range(nh)
    chains = range(nh * rep)
    head_of = [c // rep for c in chains]
    lanes = [slice(c * HEAD_DIM, (c + 1) * HEAD_DIM) for c in chains]
    q = [q_ref[:, h * HEAD_DIM:(h + 1) * HEAD_DIM] for h in heads]
    k = [k_ref[:, h * HEAD_DIM:(h + 1) * HEAD_DIM] for h in heads]
    kk = [_bdot_nt(k[h], k[h]) for h in heads]
    qk_raw = [_bdot_nt(q[h], k[h]) for h in heads]
    gcol = [gcol_ref[c // rep, :, c % rep:c % rep + 1] for c in chains]
    bcol = [bcol_ref[c // rep, :, c % rep:c % rep + 1] for c in chains]
    grow = [grow_ref[c // rep, c % rep:c % rep + 1, :] for c in chains]
    glast = [_block_rows(gcol[c], CHUNK, CHUNK - 1) for c in chains]
    decay = [jnp.exp2(jnp.where(causal, gcol[c] - grow[c], -jnp.inf)) for c in chains]
    e = _unit_lower_inverse_minus_identity(
        [jnp.where(strict, kk[head_of[c]] * bcol[c] * decay[c], 0.0) for c in chains])
    eg = [jnp.exp2(gcol[c]) for c in chains]
    rhs = [jnp.concatenate([v_all[:, lanes[c]] * bcol[c], k[head_of[c]] * (bcol[c] * eg[c])], axis=1)
           for c in chains]
    sol = [rhs[c] + _bdot(e[c], rhs[c]) for c in chains]
    qk = [jnp.where(causal, qk_raw[head_of[c]] * decay[c], 0.0) for c in chains]
    qk_sol = [_bdot(qk[c], sol[c]) for c in chains]
    q_eff = [q[head_of[c]] * eg[c] - qk_sol[c][:, HEAD_DIM:] for c in chains]
    k_dec = [k[head_of[c]] * jnp.exp2(glast[c] - gcol[c]) for c in chains]
    g_chunk = [jnp.exp2(glast[c]) for c in chains]
    state = [state_ref[c] for c in chains]
    for j in range(ts // CHUNK):
        rows = slice(j * CHUNK, (j + 1) * CHUNK)
        kt_sol = [_bdot_tn(k_dec[c][rows], sol[c][rows]) for c in chains]
        prod = [_bdot(jnp.concatenate([kt_sol[c][:, HEAD_DIM:], q_eff[c][rows]], axis=0), state[c])
                for c in chains]
        state = [state[c] * g_chunk[c][j * CHUNK:j * CHUNK + 1] - prod[c][:HEAD_DIM] + kt_sol[c][:, :HEAD_DIM]
                 for c in chains]
        for c in chains:
            o = prod[c][HEAD_DIM:] + qk_sol[c][rows, :HEAD_DIM]
            o = o * lax.rsqrt(jnp.mean(o * o, axis=-1, keepdims=True) + NORM_EPS) * nw
            o_ref[rows, lanes[c]] = (o * z_ref[rows, lanes[c]]).astype(o_ref.dtype)
    for c in chains:
        state_ref[c] = state[c]


GDN_TILE = 128
GDN_HEADS_PER_STEP = 8


def gdn_recurrence(qk, v, z_act, gcum, beta, norm_w, ts=GDN_TILE, nh=GDN_HEADS_PER_STEP):
    s = qk.shape[0]
    key_dim = qk.shape[1] // 2
    value_dim = v.shape[1]
    hq = key_dim // HEAD_DIM
    hv = value_dim // HEAD_DIM
    rep = hv // hq
    nh = min(nh, hq)
    ts = min(ts, s)
    assert hq % nh == 0 and s % ts == 0
    qw = nh * HEAD_DIM
    vw = nh * rep * HEAD_DIM
    gcol = gcum.reshape(s, hq, rep).transpose(1, 0, 2)
    bcol = beta.reshape(s, hq, rep).transpose(1, 0, 2)
    grow = gcum.T.reshape(hq, rep, s)
    k_off = key_dim // qw
    return pl.pallas_call(
        _gdn_kernel,
        grid=(hq // nh, s // ts),
        in_specs=[pl.BlockSpec((ts, qw), lambda h, i: (i, h)),
                  pl.BlockSpec((ts, qw), lambda h, i: (i, k_off + h)),
                  pl.BlockSpec((ts, vw), lambda h, i: (i, h)),
                  pl.BlockSpec((ts, vw), lambda h, i: (i, h)),
                  pl.BlockSpec((nh, ts, rep), lambda h, i: (h, i, 0)),
                  pl.BlockSpec((nh, ts, rep), lambda h, i: (h, i, 0)),
                  pl.BlockSpec((nh, rep, ts), lambda h, i: (h, 0, i)),
                  pl.BlockSpec((1, HEAD_DIM), lambda h, i: (0, 0))],
        out_specs=pl.BlockSpec((ts, vw), lambda h, i: (i, h)),
        out_shape=jax.ShapeDtypeStruct((s, value_dim), BF16),
        scratch_shapes=[pltpu.VMEM((nh * rep, HEAD_DIM, HEAD_DIM), F32)],
        compiler_params=_params("parallel", "arbitrary"),
        name="gdn_recurrence",
    )(qk, qk, v, z_act, gcol, bcol, grow, norm_w.reshape(1, HEAD_DIM))


def gated_deltanet_mixer(u, w_in, conv_w, a_log, dt_bias, norm_w, w_out):
    hv = a_log.shape[0]
    value_dim = hv * HEAD_DIM
    key_dim = (w_in.shape[1] - 2 * value_dim - 2 * hv) // 2
    c_qk, c_v, c_z = 2 * key_dim, 2 * key_dim + value_dim, 2 * key_dim + 2 * value_dim
    q_scale = jnp.concatenate([jnp.full((key_dim,), HEAD_DIM ** -0.5, F32), jnp.ones((key_dim,), F32)])
    qk = proj_act(u, w_in[:, :c_qk].astype(BF16), "conv_norm", conv_w[:, :c_qk], q_scale, name="gdn_qk_proj")
    v = proj_act(u, w_in[:, c_qk:c_v].astype(BF16), "conv", conv_w[:, c_qk:c_v], name="gdn_v_proj")
    z_act = proj_act(u, w_in[:, c_v:c_z].astype(BF16), "silu", name="gdn_z_proj")
    ab = matmul(u, w_in[:, c_z:].astype(BF16), name="gdn_gate_proj")
    gcum, beta = gdn_gates(ab, a_log, dt_bias)
    o = gdn_recurrence(qk, v, z_act, gcum, beta, norm_w)
    return matmul(o, w_out.astype(BF16), tn=512, name="gdn_out_proj")


def _block_rows(x, sub, offset):
    n = x.shape[0] // sub
    x3 = x.reshape(n, sub, x.shape[1])
    return jnp.broadcast_to(x3[:, offset:offset + 1, :], x3.shape).reshape(x.shape)


def _gla_masks():
    row = lax.broadcasted_iota(jnp.int32, (CHUNK, CHUNK), 0)
    col = lax.broadcasted_iota(jnp.int32, (CHUNK, CHUNK), 1)
    pair = {}
    size = CHUNK // 2
    while size >= GLA_SUB:
        pair[size] = ((row // size) % 2 == 1) & ((row // size) == (col // size) + 1)
        size //= 2
    same_sub = (row // GLA_SUB) == (col // GLA_SUB)
    row_in = lax.broadcasted_iota(jnp.int32, (CHUNK, 1), 0) % GLA_SUB
    diag_col = [same_sub & ((col % GLA_SUB) == j) for j in range(GLA_SUB)]
    row_from = [row_in >= j for j in range(GLA_SUB)]
    return pair, diag_col, row_from


def _gla_intra_chunk(q, k, b, masks):
    pair, diag_col, row_from = masks
    att = jnp.zeros((CHUNK, CHUNK), F32)
    for size in pair:
        b_row_ref = _block_rows(b, size, 0)
        nxt = jnp.concatenate([b_row_ref[size:], jnp.broadcast_to(b[CHUNK - 1:CHUNK], (size, b.shape[1]))], axis=0)
        qs = q * jnp.exp2(b - b_row_ref)
        ks = k * jnp.exp2(nxt - b)
        att = att + jnp.where(pair[size], _bdot_nt(qs, ks), 0.0)
    for j in range(GLA_SUB):
        bj = _block_rows(b, GLA_SUB, j)
        kj = _block_rows(k, GLA_SUB, j)
        e = jnp.exp2(jnp.where(row_from[j], b - bj, -jnp.inf))
        s = jnp.sum(q * e * kj, axis=-1, keepdims=True)
        att = jnp.where(diag_col[j], s, att)
    return att


def _hgrn_lower_bound(lbl_ref, layer):
    depth = lbl_ref.shape[0]
    rows = [lbl_ref[i:i + 1, :] for i in range(depth)]
    m = functools.reduce(jnp.maximum, rows)
    ex = [jnp.exp(r - m) for r in rows]
    total = functools.reduce(lambda x, y: x + y, ex)
    acc = jnp.zeros_like(m)
    for i in range(1, layer + 1):
        acc = acc + ex[i] / total
    return acc


def _hgrn_kernel(q_ref, f_ref, i_ref, og_ref, lbl_ref, nw_ref, o_ref, state_ref, *, layer):
    ts = q_ref.shape[0]

    @pl.when(pl.program_id(1) == 0)
    def _():
        state_ref[...] = jnp.zeros(state_ref.shape, F32)

    lb = _hgrn_lower_bound(lbl_ref, layer)
    f = f_ref[...]
    log_sig = jnp.minimum(f, 0.0) - jnp.log(1.0 + jnp.exp(-jnp.abs(f)))
    t0 = jnp.log(lb)
    t1 = jnp.log1p(-lb) + log_sig
    log_f = jnp.maximum(t0, t1) + jnp.log(1.0 + jnp.exp(-jnp.abs(t0 - t1)))
    q = _silu(q_ref[...])
    k = (1.0 - lb) * _sigmoid(-f)
    row = lax.broadcasted_iota(jnp.int32, (CHUNK, CHUNK), 0)
    col = lax.broadcasted_iota(jnp.int32, (CHUNK, CHUNK), 1)
    tri = (row >= col).astype(F32)
    nw = nw_ref[...]
    nh = q.shape[1] // HEAD_DIM
    heads = range(nh)
    units = [(h, j) for j in range(ts // CHUNK) for h in heads]
    view = {(h, j): (slice(j * CHUNK, (j + 1) * CHUNK), slice(h * HEAD_DIM, (h + 1) * HEAD_DIM)) for h, j in units}
    log2_f = log_f * LOG2_E
    b = {u: jnp.dot(tri, log2_f[view[u]], preferred_element_type=F32, precision=lax.Precision.HIGHEST)
         for u in units}
    masks = _gla_masks()
    att = {u: _gla_intra_chunk(q[view[u]], k[view[u]], b[u], masks) for u in units}
    o_intra = {u: _bdot(att[u], i_ref[view[u]]) for u in units}
    q_dec = {u: q[view[u]] * jnp.exp2(b[u]) for u in units}
    k_dec = {u: k[view[u]] * jnp.exp2(b[u][CHUNK - 1:CHUNK, :] - b[u]) for u in units}
    kv = {u: _bdot_tn(i_ref[view[u]], k_dec[u]) for u in units}
    state_t = [state_ref[h] for h in heads]
    for j in range(ts // CHUNK):
        o = [o_intra[h, j] + _bdot_nt(q_dec[h, j], state_t[h]) for h in heads]
        state_t = [state_t[h] * jnp.exp2(b[h, j][CHUNK - 1:CHUNK, :]) + kv[h, j] for h in heads]
        for h in heads:
            oh = o[h] * lax.rsqrt(jnp.mean(o[h] * o[h], axis=-1, keepdims=True) + NORM_EPS) * nw
            o_ref[view[h, j]] = (oh * _silu(og_ref[view[h, j]])).astype(o_ref.dtype)
    for h in heads:
        state_ref[h] = state_t[h]


HGRN_TILE = 256
HGRN_HEADS_PER_STEP = 4


def hgrn_recurrence(proj, lb_logits, layer, norm_w, ts=HGRN_TILE, nh=HGRN_HEADS_PER_STEP):
    s, n = proj.shape
    d = n // 4
    heads = d // HEAD_DIM
    ts = min(ts, s)
    nh = min(nh, heads)
    assert heads % nh == 0 and s % ts == 0
    depth = lb_logits.shape[0]
    groups = heads // nh
    width = nh * HEAD_DIM

    def col(off):
        return pl.BlockSpec((ts, width), lambda h, i: (i, off * groups + h))

    return pl.pallas_call(
        functools.partial(_hgrn_kernel, layer=layer),
        grid=(groups, s // ts),
        in_specs=[col(0), col(1), col(2), col(3),
                  pl.BlockSpec((depth, width), lambda h, i: (0, h)),
                  pl.BlockSpec((1, HEAD_DIM), lambda h, i: (0, 0))],
        out_specs=pl.BlockSpec((ts, width), lambda h, i: (i, h)),
        out_shape=jax.ShapeDtypeStruct((s, d), BF16),
        scratch_shapes=[pltpu.VMEM((nh, HEAD_DIM, HEAD_DIM), F32)],
        compiler_params=_params("parallel", "arbitrary"),
        name="hgrn_recurrence",
    )(proj, proj, proj, proj, lb_logits, norm_w.reshape(1, HEAD_DIM))


def hgrn2_mixer(u, w_in, lb_logits, layer, norm_w, w_out):
    proj = matmul(u, w_in.astype(BF16), name="hgrn_in_proj")
    o = hgrn_recurrence(proj, lb_logits, layer, norm_w)
    return matmul(o, w_out.astype(BF16), name="hgrn_out_proj")


ROUTER_LANES = 128
LANE_SENTINEL = 1 << 20


def _router_kernel(h_ref, mod_ref, w_ref, eid_ref, gate_ref, rank_ref, count_ref, run_ref, *,
                   shift_row, scale_row, groups, epg_shift):
    @pl.when(pl.program_id(0) == 0)
    def _():
        run_ref[...] = jnp.zeros(run_ref.shape, F32)

    shift = mod_ref[shift_row:shift_row + 1, :]
    scale = mod_ref[scale_row:scale_row + 1, :]
    u = h_ref[...] * (1.0 + scale) + shift
    logits = jnp.dot(u, w_ref[...], preferred_element_type=F32, precision=lax.Precision.HIGHEST)
    lane = lax.broadcasted_iota(jnp.int32, logits.shape, 1)
    n_exp = groups << epg_shift

    def first_argmax(vals):
        m = jnp.max(vals, axis=-1, keepdims=True)
        idx = jnp.min(jnp.where(vals == m, lane, LANE_SENTINEL), axis=-1, keepdims=True)
        return m, idx

    is_group = lane < groups
    gmax, grp = first_argmax(jnp.where(is_group, logits, -jnp.inf))
    p_grp = 1.0 / jnp.sum(jnp.where(is_group, jnp.exp(logits - gmax), 0.0), axis=-1, keepdims=True)
    e_lane = lane - groups
    in_grp = (e_lane >= 0) & (e_lane < n_exp) & (lax.shift_right_arithmetic(e_lane, epg_shift) == grp)
    el = jnp.where(in_grp, logits, -jnp.inf)
    m1, i1 = first_argmax(el)
    z = jnp.sum(jnp.where(in_grp, jnp.exp(logits - m1), 0.0), axis=-1, keepdims=True)
    m2, i2 = first_argmax(jnp.where(lane == i1, -jnp.inf, el))
    p1 = 1.0 / z
    p2 = jnp.exp(m2 - m1) / z
    denom = p1 + p2
    e1, e2 = i1 - groups, i2 - groups
    eid_ref[...] = jnp.where(lane == 0, e1, jnp.where(lane == 1, e2, 0))
    gate_ref[...] = jnp.where(lane == 0, p_grp * p1 / denom, jnp.where(lane == 1, p_grp * p2 / denom, 0.0))

    tm = logits.shape[0]
    tri = (lax.broadcasted_iota(jnp.int32, (tm, tm), 0) >= lax.broadcasted_iota(jnp.int32, (tm, tm), 1)).astype(BF16)
    hot1 = (lane == e1).astype(F32)
    hot2 = (lane == e2).astype(F32)
    cum1 = jnp.dot(tri, hot1.astype(BF16), preferred_element_type=F32)
    cum2 = jnp.dot(tri, hot2.astype(BF16), preferred_element_type=F32)
    run = run_ref[...]
    after1 = run + cum1[tm - 1:tm, :]
    rank1 = jnp.sum(hot1 * (run + cum1), axis=-1, keepdims=True) - 1.0
    rank2 = jnp.sum(hot2 * (after1 + cum2), axis=-1, keepdims=True) - 1.0
    total = after1 + cum2[tm - 1:tm, :]
    run_ref[...] = total
    count_ref[...] = total.astype(jnp.int32)
    rank_ref[...] = jnp.where(lane == 0, rank1, jnp.where(lane == 1, rank2, 0.0)).astype(jnp.int32)


def moe_router(h, mod, shift_row, scale_row, w_group, w_expert, tm=256):
    s, d = h.shape
    groups = w_group.shape[1]
    n_exp = w_expert.shape[1]
    epg = n_exp // groups
    assert epg & (epg - 1) == 0 and groups + n_exp <= ROUTER_LANES
    w = jnp.concatenate([w_group, w_expert, jnp.zeros((d, ROUTER_LANES - groups - n_exp), F32)], axis=1)
    out = pl.BlockSpec((tm, ROUTER_LANES), lambda i: (i, 0))
    ints = jax.ShapeDtypeStruct((s, ROUTER_LANES), jnp.int32)
    eid, gate, rank, counts = pl.pallas_call(
        functools.partial(_router_kernel, shift_row=shift_row, scale_row=scale_row, groups=groups,
                          epg_shift=epg.bit_length() - 1),
        grid=(s // tm,),
        in_specs=[pl.BlockSpec((tm, d), lambda i: (i, 0)),
                  pl.BlockSpec(mod.shape, lambda i: (0, 0)),
                  pl.BlockSpec((d, ROUTER_LANES), lambda i: (0, 0))],
        out_specs=[out, out, out, pl.BlockSpec((1, ROUTER_LANES), lambda i: (0, 0))],
        out_shape=[ints, jax.ShapeDtypeStruct((s, ROUTER_LANES), F32), ints,
                   jax.ShapeDtypeStruct((1, ROUTER_LANES), jnp.int32)],
        scratch_shapes=[pltpu.VMEM((1, ROUTER_LANES), F32)],
        compiler_params=_params("arbitrary"),
        name="moe_router",
    )(h, mod, w)
    return eid[:, :MOE_TOP_K], gate[:, :MOE_TOP_K], rank[:, :MOE_TOP_K], counts[0, :n_exp]


def _dispatch_plan(eid, rank, counts):
    s = eid.shape[0]
    n_exp = counts.shape[0]
    a = s * MOE_TOP_K
    padded = (counts + MOE_ROWS - 1) // MOE_ROWS * MOE_ROWS
    pad_end = jnp.cumsum(padded)
    pad_start = pad_end - padded
    dest = pad_start[eid] + rank
    n_blk = -(-a // MOE_ROWS) + n_exp
    tok_buf = jnp.zeros((n_blk * MOE_ROWS,), jnp.int32).at[dest.reshape(a)].set(
        jnp.arange(a, dtype=jnp.int32) // MOE_TOP_K)
    blk_start = jnp.arange(n_blk, dtype=jnp.int32) * MOE_ROWS
    blk_e = jnp.minimum(jnp.searchsorted(pad_end, blk_start, side="right"), n_exp - 1).astype(jnp.int32)
    blk_used = (blk_start < pad_end[-1]).astype(jnp.int32)
    return tok_buf, blk_e, blk_used, dest


def _for_rows(n, fn):
    for r in range(n):
        fn(r)


def _moe_ffn_kernel(blk_e_ref, blk_used_ref, tok_cur, tok_next, up_hbm, wg_ref, wu_ref, wd_ref, y_ref,
                    xbuf, sem, w_in_bf, wd_bf, prod_ref):
    f, d = wd_bf.shape
    half = d // 2
    b = pl.program_id(0)
    n = pl.num_programs(0)
    slot = b % 2
    used = blk_used_ref[b] > 0
    next_used = blk_used_ref[jnp.minimum(b + 1, n - 1)] > 0

    def row_copy(tok_ref, to_slot, r):
        return pltpu.make_async_copy(up_hbm.at[tok_ref[0, 0, r]], xbuf.at[to_slot, r], sem.at[to_slot])

    @pl.when((b == 0) & used)
    def _():
        _for_rows(MOE_ROWS, lambda r: row_copy(tok_cur, 0, r).start(priority=r % DMA_QUEUES))

    @pl.when((b + 1 < n) & next_used)
    def _():
        _for_rows(MOE_ROWS, lambda r: row_copy(tok_next, 1 - slot, r).start(priority=r % DMA_QUEUES))

    @pl.when(used & ((b == 0) | (blk_e_ref[b] != blk_e_ref[jnp.maximum(b - 1, 0)])))
    def _():
        w_in_bf[:, 0 * f:1 * f] = wg_ref[0, 0, 0:half, :].astype(BF16)
        w_in_bf[:, 1 * f:2 * f] = wg_ref[0, 0, half:d, :].astype(BF16)
        w_in_bf[:, 2 * f:3 * f] = wu_ref[0, 0, 0:half, :].astype(BF16)
        w_in_bf[:, 3 * f:4 * f] = wu_ref[0, 0, half:d, :].astype(BF16)
        wd_bf[...] = wd_ref[0, 0].astype(BF16)

    @pl.when(used)
    def _():
        _for_rows(MOE_ROWS, lambda r: row_copy(tok_cur, slot, r).wait())
        x2 = jnp.concatenate([pltpu.bitcast(xbuf[slot, :, j, :], BF16) for j in range(xbuf.shape[2])], axis=1)
        prod = jnp.dot(x2, w_in_bf[...], preferred_element_type=F32)
        tiles = f // HEAD_DIM
        for t in range(4 * tiles):
            prod_ref[t] = prod[:, t * HEAD_DIM:(t + 1) * HEAD_DIM]

        def rows(parity, first_tile):
            return jnp.concatenate([prod_ref[first_tile + t, pl.ds(parity, MOE_ROWS, stride=2), :]
                                    for t in range(tiles)], axis=1)

        g = rows(0, 0) + rows(1, tiles)
        up = rows(0, 2 * tiles) + rows(1, 3 * tiles)
        hid = (_silu(g) * up).astype(BF16)
        y = jnp.dot(hid, wd_bf[...], preferred_element_type=F32)
        for j in range(y_ref.shape[1]):
            y_ref[:, j, :] = y[:, j * HEAD_DIM:(j + 1) * HEAD_DIM]

    @pl.when(jnp.logical_not(used))
    def _():
        y_ref[...] = jnp.zeros(y_ref.shape, F32)


def moe_ffn(up, tok_buf, blk_e, blk_used, w_gate, w_up, w_down, layer):
    p = tok_buf.shape[0]
    d, f = w_gate.shape[-2:]
    n_blk = p // MOE_ROWS
    slabs_in = up.shape[1]
    tok3 = tok_buf.reshape(n_blk, 1, MOE_ROWS)
    grid_spec = pltpu.PrefetchScalarGridSpec(
        num_scalar_prefetch=2,
        grid=(n_blk,),
        in_specs=[pl.BlockSpec((1, 1, MOE_ROWS), lambda b, e, u: (b, 0, 0), memory_space=pltpu.SMEM),
                  pl.BlockSpec((1, 1, MOE_ROWS), lambda b, e, u: (jnp.minimum(b + 1, n_blk - 1), 0, 0),
                               memory_space=pltpu.SMEM),
                  pl.BlockSpec(memory_space=pl.ANY),
                  pl.BlockSpec((1, 1, d, f), lambda b, e, u: (layer, e[b], 0, 0)),
                  pl.BlockSpec((1, 1, d, f), lambda b, e, u: (layer, e[b], 0, 0)),
                  pl.BlockSpec((1, 1, f, d), lambda b, e, u: (layer, e[b], 0, 0))],
        out_specs=pl.BlockSpec((MOE_ROWS, d // HEAD_DIM, HEAD_DIM), lambda b, e, u: (b, 0, 0)),
        scratch_shapes=[pltpu.VMEM((2, MOE_ROWS, slabs_in, HEAD_DIM), jnp.uint32), pltpu.SemaphoreType.DMA((2,)),
                        pltpu.VMEM((d // 2, 4 * f), BF16), pltpu.VMEM((f, d), BF16),
                        pltpu.VMEM((4 * f // HEAD_DIM, 2 * MOE_ROWS, HEAD_DIM), F32)],
    )
    return pl.pallas_call(
        _moe_ffn_kernel,
        grid_spec=grid_spec,
        out_shape=jax.ShapeDtypeStruct((p, d // HEAD_DIM, HEAD_DIM), F32),
        compiler_params=_params("arbitrary"),
        name="moe_ffn",
    )(blk_e, blk_used, tok3, tok3, up, w_gate, w_up, w_down)


def _combine_kernel(pos_cur, pos_next, h_ref, gw_ref, mod_ref, lnw_ref, lnb_ref, ys_hbm, *rest,
                    gate_row, shift_row, scale_row, alpha):
    out_refs, (buf, sem) = rest[:-2], rest[-2:]
    tm = h_ref.shape[0]
    i = pl.program_id(0)
    n = pl.num_programs(0)
    slot = i % 2

    def row_copy(pos_ref, to_slot, k, t):
        return pltpu.make_async_copy(ys_hbm.at[pos_ref[0, 0, k * tm + t]], buf.at[to_slot, k, t], sem.at[to_slot])

    def for_all_rows(fn):
        for k in range(MOE_TOP_K):
            _for_rows(tm, functools.partial(fn, k))

    @pl.when(i == 0)
    def _():
        for_all_rows(lambda k, t: row_copy(pos_cur, 0, k, t).start(priority=t % DMA_QUEUES))

    @pl.when(i + 1 < n)
    def _():
        for_all_rows(lambda k, t: row_copy(pos_next, 1 - slot, k, t).start(priority=t % DMA_QUEUES))

    for_all_rows(lambda k, t: row_copy(pos_cur, slot, k, t).wait())

    pieces = []
    for j in range(buf.shape[3]):
        piece = gw_ref[:, 0:1] * buf[slot, 0, :, j, :]
        for k in range(1, MOE_TOP_K):
            piece = piece + gw_ref[:, k:k + 1] * buf[slot, k, :, j, :]
        pieces.append(piece)
    y = jnp.concatenate(pieces, axis=1)
    gate = mod_ref[gate_row:gate_row + 1, :]
    hn = _residual_norm(h_ref[...], y, gate, lnw_ref[...], lnb_ref[...], alpha)
    out_refs[0][...] = hn
    if shift_row is not None:
        shift = mod_ref[shift_row:shift_row + 1, :]
        scale = mod_ref[scale_row:scale_row + 1, :]
        out_refs[1][...] = (hn * (1.0 + scale) + shift).astype(BF16)


def moe_combine_norm(h, ys, pos, gate_w, mod, lnw, lnb, gate_row, shift_row, scale_row, alpha, tm=128):
    s, d = h.shape
    tm = min(tm, s)
    n = s // tm
    pos_tiles = pos.reshape(n, tm, MOE_TOP_K).transpose(0, 2, 1).reshape(n, 1, MOE_TOP_K * tm)
    row = pl.BlockSpec((tm, d), lambda i: (i, 0))
    vec = pl.BlockSpec((1, d), lambda i: (0, 0))
    with_u = shift_row is not None
    out_shape = [jax.ShapeDtypeStruct((s, d), F32)] + ([jax.ShapeDtypeStruct((s, d), BF16)] if with_u else [])
    outs = pl.pallas_call(
        functools.partial(_combine_kernel, gate_row=gate_row, shift_row=shift_row, scale_row=scale_row, alpha=alpha),
        grid=(n,),
        in_specs=[pl.BlockSpec((1, 1, MOE_TOP_K * tm), lambda i: (i, 0, 0), memory_space=pltpu.SMEM),
                  pl.BlockSpec((1, 1, MOE_TOP_K * tm), lambda i: (jnp.minimum(i + 1, n - 1), 0, 0),
                               memory_space=pltpu.SMEM),
                  row,
                  pl.BlockSpec((tm, MOE_TOP_K), lambda i: (i, 0)),
                  pl.BlockSpec(mod.shape, lambda i: (0, 0)), vec, vec,
                  pl.BlockSpec(memory_space=pl.ANY)],
        out_specs=[row] * len(out_shape),
        out_shape=out_shape,
        scratch_shapes=[pltpu.VMEM((2, MOE_TOP_K, tm) + ys.shape[1:], F32), pltpu.SemaphoreType.DMA((2,))],
        compiler_params=_params("arbitrary"),
        name="moe_combine_norm",
    )(pos_tiles, pos_tiles, h, gate_w, mod, lnw.reshape(1, d), lnb.reshape(1, d), ys)
    return (outs[0], outs[1]) if with_u else (outs[0], None)


def moe_sublayer(h, up, mod, lnw, lnb, w_group, w_expert, w_gate, w_up, w_down, layer,
                 cur_shift_row, cur_scale_row, gate_row, shift_row, scale_row, alpha):
    eid, gate_w, rank, counts = moe_router(h, mod, cur_shift_row, cur_scale_row, w_group, w_expert)
    tok_buf, blk_e, blk_used, pos = _dispatch_plan(eid, rank, counts)
    ys = moe_ffn(up, tok_buf, blk_e, blk_used, w_gate, w_up, w_down, layer)
    return moe_combine_norm(h, ys, pos, gate_w, mod, lnw, lnb, gate_row, shift_row, scale_row, alpha)


def kernel(x, c, ada_w, ada_b, ln_w, ln_b, gdn_w_in, gdn_conv_w, gdn_a_log, gdn_dt_bias, gdn_norm_w, gdn_w_out,
           hgrn_w_in, hgrn_lb_logits, hgrn_norm_w, hgrn_w_out, moe_w_group, moe_w_expert, moe_w_gate, moe_w_up,
           moe_w_down):
    batch, s, d = x.shape
    assert batch == 1, "one sequence per call"
    depth = ada_w.shape[0]
    alpha = (2 * depth) ** 0.25
    mod = ada_modulation(c, ada_w, ada_b)
    h = x.reshape(s, d)
    u = modulate(h, mod, 0, 1)
    for layer in range(depth):
        base = 6 * layer
        j = layer // 2
        if layer % 2 == 0:
            y = gated_deltanet_mixer(u, gdn_w_in[j], gdn_conv_w[j], gdn_a_log[j], gdn_dt_bias[j], gdn_norm_w[j],
                                     gdn_w_out[j])
        else:
            y = hgrn2_mixer(u, hgrn_w_in[j], hgrn_lb_logits, layer, hgrn_norm_w[j], hgrn_w_out[j])
        h, u = residual_norm(h, y, mod, ln_w[layer, 0], ln_b[layer, 0], base + 2, base + 3, base + 4, alpha)
        last = layer + 1 == depth
        h, u = moe_sublayer(h, u, mod, ln_w[layer, 1], ln_b[layer, 1], moe_w_group[layer], moe_w_expert[layer],
                            moe_w_gate, moe_w_up, moe_w_down, layer,
                            cur_shift_row=base + 3, cur_scale_row=base + 4, gate_row=base + 5,
                            shift_row=None if last else base + 6, scale_row=None if last else base + 7, alpha=alpha)
    return h.reshape(batch, s, d)
```

```python
import functools

import jax
import jax.numpy as jnp
from jax import lax
from jax.experimental import pallas as pl
from jax.experimental.pallas import tpu as pltpu

F32 = jnp.float32
BF16 = jnp.bfloat16

HEAD_DIM = 128
CHUNK = 64
SUB = 16
GLA_SUB = 8
CONV_WIDTH = 4
CONV_HALO = 8
NORM_EPS = 1e-6
LOG2_E = 1.4426950408889634
MOE_TOP_K = 2
MOE_ROWS = 256
DMA_QUEUES = 2
V7X_VMEM_BYTES = 64 * 1024 * 1024
VMEM_LIMIT = V7X_VMEM_BYTES - 8 * 1024 * 1024


def _params(*sem, vmem=VMEM_LIMIT):
    return pltpu.CompilerParams(dimension_semantics=sem, vmem_limit_bytes=vmem)


def _sigmoid(x):
    return 1.0 / (1.0 + jnp.exp(-x))


def _silu(x):
    return x * _sigmoid(x)


def _softplus(x):
    return jnp.maximum(x, 0.0) + jnp.log1p(jnp.exp(-jnp.abs(x)))


def _bdot(a, b):
    return jnp.dot(a.astype(BF16), b.astype(BF16), preferred_element_type=F32)


def _bdot_nt(a, b):
    return lax.dot_general(a.astype(BF16), b.astype(BF16), (((1,), (1,)), ((), ())),
                           preferred_element_type=F32)


def _bdot_tn(a, b):
    return lax.dot_general(a.astype(BF16), b.astype(BF16), (((0,), (0,)), ((), ())),
                           preferred_element_type=F32)


def _ada_kernel(c_ref, w_ref, b_ref, o_ref):
    cond = _silu(c_ref[...])
    o_ref[0] = jnp.sum(cond * w_ref[0], axis=0, keepdims=True) + b_ref[0]


def ada_modulation(c, ada_w, ada_b, tn=512):
    depth, d, n = ada_w.shape
    out = pl.pallas_call(
        _ada_kernel,
        grid=(depth, n // tn),
        in_specs=[pl.BlockSpec((d, 1), lambda l, j: (0, 0)),
                  pl.BlockSpec((1, d, tn), lambda l, j: (l, 0, j)),
                  pl.BlockSpec((1, 1, tn), lambda l, j: (l, 0, j))],
        out_specs=pl.BlockSpec((1, 1, tn), lambda l, j: (l, 0, j)),
        out_shape=jax.ShapeDtypeStruct((depth, 1, n), F32),
        compiler_params=_params("parallel", "parallel"),
        name="ada_modulation",
    )(c.reshape(d, 1), ada_w, ada_b.reshape(depth, 1, n))
    return out.reshape(depth * 6, d)


def _modulate_kernel(x_ref, mod_ref, u_ref, *, shift_row, scale_row):
    shift = mod_ref[shift_row:shift_row + 1, :]
    scale = mod_ref[scale_row:scale_row + 1, :]
    u_ref[...] = (x_ref[...] * (1.0 + scale) + shift).astype(u_ref.dtype)


def modulate(x, mod, shift_row, scale_row, tm=256):
    s, d = x.shape
    return pl.pallas_call(
        functools.partial(_modulate_kernel, shift_row=shift_row, scale_row=scale_row),
        grid=(s // tm,),
        in_specs=[pl.BlockSpec((tm, d), lambda i: (i, 0)),
                  pl.BlockSpec(mod.shape, lambda i: (0, 0))],
        out_specs=pl.BlockSpec((tm, d), lambda i: (i, 0)),
        out_shape=jax.ShapeDtypeStruct((s, d), BF16),
        compiler_params=_params("parallel"),
        name="modulate",
    )(x, mod)


def _mm_kernel(x_ref, w_ref, o_ref):
    o_ref[...] = jnp.dot(x_ref[...], w_ref[...], preferred_element_type=F32).astype(o_ref.dtype)


def matmul(x, w, tm=1024, tn=1024, out_dtype=F32, name="matmul"):
    m, k = x.shape
    _, n = w.shape
    tm, tn = min(tm, m), min(tn, n)
    while n % tn:
        tn -= HEAD_DIM
    assert m % tm == 0 and tn > 0, (m, n, tm, tn)
    return pl.pallas_call(
        _mm_kernel,
        grid=(n // tn, m // tm),
        in_specs=[pl.BlockSpec((tm, k), lambda j, i: (i, 0)),
                  pl.BlockSpec((k, tn), lambda j, i: (0, j))],
        out_specs=pl.BlockSpec((tm, tn), lambda j, i: (i, j)),
        out_shape=jax.ShapeDtypeStruct((m, n), out_dtype),
        compiler_params=_params("parallel", "parallel"),
        name=name,
    )(x, w)


def _residual_norm(h, y, gate, lnw, lnb, alpha):
    v = alpha * h + (1.0 + gate) * y
    mu = jnp.mean(v, axis=-1, keepdims=True)
    dv = v - mu
    var = jnp.mean(dv * dv, axis=-1, keepdims=True)
    return dv * lax.rsqrt(var + NORM_EPS) * lnw + lnb


HI_HALF_MASK = 0xFFFF0000


def _pack_bf16_halves(u):
    half = u.shape[1] // 2
    lo = pltpu.bitcast(u[:, :half].astype(BF16).astype(F32), jnp.uint32)
    hi = pltpu.bitcast(u[:, half:].astype(BF16).astype(F32), jnp.uint32)
    return (lo >> 16) | (hi & jnp.uint32(HI_HALF_MASK))


def _unpack_bf16_halves(words):
    lo = pltpu.bitcast(words << 16, F32).astype(BF16)
    hi = pltpu.bitcast(words & jnp.uint32(HI_HALF_MASK), F32).astype(BF16)
    return lo, hi


def _ln_kernel(h_ref, y_ref, mod_ref, lnw_ref, lnb_ref, h_out, up_out, *, gate_row, shift_row, scale_row, alpha):
    gate = mod_ref[gate_row:gate_row + 1, :]
    hn = _residual_norm(h_ref[...], y_ref[...], gate, lnw_ref[...], lnb_ref[...], alpha)
    h_out[...] = hn
    shift = mod_ref[shift_row:shift_row + 1, :]
    scale = mod_ref[scale_row:scale_row + 1, :]
    words = _pack_bf16_halves(hn * (1.0 + scale) + shift)
    for j in range(up_out.shape[1]):
        up_out[:, j, :] = words[:, j * HEAD_DIM:(j + 1) * HEAD_DIM]


def residual_norm(h, y, mod, lnw, lnb, gate_row, shift_row, scale_row, alpha, tm=256):
    s, d = h.shape
    row = pl.BlockSpec((tm, d), lambda i: (i, 0))
    vec = pl.BlockSpec((1, d), lambda i: (0, 0))
    slabs = d // (2 * HEAD_DIM)
    return pl.pallas_call(
        functools.partial(_ln_kernel, gate_row=gate_row, shift_row=shift_row, scale_row=scale_row, alpha=alpha),
        grid=(s // tm,),
        in_specs=[row, row, pl.BlockSpec(mod.shape, lambda i: (0, 0)), vec, vec],
        out_specs=[row, pl.BlockSpec((tm, slabs, HEAD_DIM), lambda i: (i, 0, 0))],
        out_shape=[jax.ShapeDtypeStruct((s, d), F32), jax.ShapeDtypeStruct((s, slabs, HEAD_DIM), jnp.uint32)],
        compiler_params=_params("parallel"),
        name="residual_norm",
    )(h, y, mod, lnw.reshape(1, d), lnb.reshape(1, d))


def _gdn_gate_kernel(ab_ref, alog_ref, dtb_ref, gcum_ref, beta_ref):
    hv = alog_ref.shape[-1]
    a = ab_ref[:, :hv]
    b = ab_ref[:, hv:]
    g = -jnp.exp(alog_ref[...]) * _softplus(a + dtb_ref[...]) * LOG2_E
    beta_ref[...] = _sigmoid(b)
    tm = g.shape[0]
    row = lax.broadcasted_iota(jnp.int32, (CHUNK, CHUNK), 0)
    col = lax.broadcasted_iota(jnp.int32, (CHUNK, CHUNK), 1)
    tri = (row >= col).astype(F32)
    for c in range(tm // CHUNK):
        gcum_ref[c * CHUNK:(c + 1) * CHUNK, :] = jnp.dot(
            tri, g[c * CHUNK:(c + 1) * CHUNK, :], preferred_element_type=F32, precision=lax.Precision.HIGHEST)


def gdn_gates(ab, a_log, dt_bias, tm=512):
    s, two_hv = ab.shape
    hv = two_hv // 2
    tm = min(tm, s)
    vec = pl.BlockSpec((1, hv), lambda i: (0, 0))
    out = pl.BlockSpec((tm, hv), lambda i: (i, 0))
    return pl.pallas_call(
        _gdn_gate_kernel,
        grid=(s // tm,),
        in_specs=[pl.BlockSpec((tm, two_hv), lambda i: (i, 0)), vec, vec],
        out_specs=[out, out],
        out_shape=[jax.ShapeDtypeStruct((s, hv), F32)] * 2,
        compiler_params=_params("parallel"),
        name="gdn_gates",
    )(ab, a_log.reshape(1, hv), dt_bias.reshape(1, hv))


def _l2norm(x):
    return x * lax.rsqrt(jnp.sum(x * x, axis=-1, keepdims=True) + NORM_EPS)


PROJ_SUB = 256


def _proj_act_kernel(x_ref, w_ref, *rest, mode):
    tn = w_ref.shape[1]
    n_sub = tn // PROJ_SUB
    if mode == "silu":
        (o_ref,) = rest
        cbufs = ()
    elif mode == "conv":
        cw_ref, o_ref, *cbufs = rest
    else:
        cw_ref, sc_ref, o_ref, *cbufs = rest
    tm = o_ref.shape[0]
    x = x_ref[...]

    if mode != "silu":
        @pl.when(pl.program_id(1) == 0)
        def _():
            for cbuf in cbufs:
                cbuf[0:CONV_HALO, :] = jnp.zeros((CONV_HALO, PROJ_SUB), F32)

    def matmul_sub(c):
        cols = slice(c * PROJ_SUB, (c + 1) * PROJ_SUB)
        acc = jnp.dot(x, w_ref[:, cols], preferred_element_type=F32)
        if mode == "silu":
            return acc
        cbufs[c][CONV_HALO:CONV_HALO + tm, :] = acc
        return None

    def epilogue(c, acc):
        cols = slice(c * PROJ_SUB, (c + 1) * PROJ_SUB)
        if mode == "silu":
            o_ref[:, cols] = _silu(acc)
            return
        cbuf = cbufs[c]
        conv = None
        for j in range(CONV_WIDTH):
            start = CONV_HALO - (CONV_WIDTH - 1) + j
            term = cw_ref[j:j + 1, cols] * cbuf[start:start + tm, :]
            conv = term if conv is None else conv + term
        cbuf[0:CONV_HALO, :] = cbuf[tm:tm + CONV_HALO, :]
        act = _silu(conv)
        if mode == "conv":
            o_ref[:, cols] = act
            return
        for h in range(PROJ_SUB // HEAD_DIM):
            lanes = slice(c * PROJ_SUB + h * HEAD_DIM, c * PROJ_SUB + (h + 1) * HEAD_DIM)
            o_ref[:, lanes] = _l2norm(act[:, h * HEAD_DIM:(h + 1) * HEAD_DIM]) * sc_ref[:, lanes]

    pending = None
    for c in range(n_sub):
        if pending is not None:
            epilogue(*pending)
        pending = (c, matmul_sub(c))
    epilogue(*pending)


def proj_act(x, w, mode, conv_w=None, col_scale=None, tm=1024, tn=1024, name="proj_act"):
    m, k = x.shape
    n = w.shape[1]
    tm, tn = min(tm, m), min(tn, n)
    assert m % tm == 0 and n % tn == 0 and tn % PROJ_SUB == 0
    col = pl.BlockSpec((1, tn), lambda j, i: (0, j))
    in_specs = [pl.BlockSpec((tm, k), lambda j, i: (i, 0)), pl.BlockSpec((k, tn), lambda j, i: (0, j))]
    args = [x, w]
    scratch = []
    if mode != "silu":
        in_specs.append(pl.BlockSpec((CONV_WIDTH, tn), lambda j, i: (0, j)))
        args.append(conv_w)
        scratch += [pltpu.VMEM((tm + CONV_HALO, PROJ_SUB), F32)] * (tn // PROJ_SUB)
    if mode == "conv_norm":
        in_specs.append(col)
        args.append(col_scale.reshape(1, n))
    return pl.pallas_call(
        functools.partial(_proj_act_kernel, mode=mode),
        grid=(n // tn, m // tm),
        in_specs=in_specs,
        out_specs=pl.BlockSpec((tm, tn), lambda j, i: (i, j)),
        out_shape=jax.ShapeDtypeStruct((m, n), F32),
        scratch_shapes=scratch,
        compiler_params=_params("arbitrary", "arbitrary"),
        name=name,
    )(*args)


def _unit_lower_inverse_minus_identity(mats):
    n = range(len(mats))
    row = lax.broadcasted_iota(jnp.int32, mats[0].shape, 0)
    col = lax.broadcasted_iota(jnp.int32, mats[0].shape, 1)
    same_sub = (row // SUB) == (col // SUB)
    b1 = [jnp.where(same_sub, -a, 0.0) for a in mats]
    bp = _products(b1, b1)
    b1bp = _products(b1, bp)
    e = [b1[i] + bp[i] + b1bp[i] for i in n]
    width = 4
    while width < SUB:
        bp = _products(bp, bp)
        ebp = _products(e, bp)
        e = [e[i] + bp[i] + ebp[i] for i in n]
        width *= 2
    size = SUB
    while size < CHUNK:
        lower_pair = ((row // (2 * size)) == (col // (2 * size))) & ((row // size) > (col // size))
        f = [jnp.where(lower_pair, a, 0.0) for a in mats]
        ef = _products(e, f)
        g = [f[i] + ef[i] for i in n]
        ge = _products(g, e)
        e = [e[i] - (g[i] + ge[i]) for i in n]
        size *= 2
    return e


def _products(xs, ys):
    return [_bdot(x, y) for x, y in zip(xs, ys)]


def _gdn_kernel(q_ref, k_ref, v_ref, z_ref, gcol_ref, bcol_ref, grow_ref, nw_ref, o_ref, state_ref):
    ts = q_ref.shape[0]

    @pl.when(pl.program_id(1) == 0)
    def _():
        state_ref[...] = jnp.zeros(state_ref.shape, F32)

    v_all = v_ref[...]
    nh = q_ref.shape[1] // HEAD_DIM
    rep = v_all.shape[1] // q_ref.shape[1]

    row = lax.broadcasted_iota(jnp.int32, (ts, ts), 0)
    col = lax.broadcasted_iota(jnp.int32, (ts, ts), 1)
    same_chunk = (row // CHUNK) == (col // CHUNK)
    causal = same_chunk & (row >= col)
    strict = same_chunk & (row > col)
    nw = nw_ref[...]

    heads = range(nh)
    chains = range(nh * rep)
    head_of = [c // rep for c in chains]
    lanes = [slice(c * HEAD_DIM, (c + 1) * HEAD_DIM) for c in chains]
    q = [q_ref[:, h * HEAD_DIM:(h + 1) * HEAD_DIM] for h in heads]
    k = [k_ref[:, h * HEAD_DIM:(h + 1) * HEAD_DIM] for h in heads]
    kk = [_bdot_nt(k[h], k[h]) for h in heads]
    qk_raw = [_bdot_nt(q[h], k[h]) for h in heads]
    gcol = [gcol_ref[c // rep, :, c % rep:c % rep + 1] for c in chains]
    bcol = [bcol_ref[c // rep, :, c % rep:c % rep + 1] for c in chains]
    grow = [grow_ref[c // rep, c % rep:c % rep + 1, :] for c in chains]
    glast = [_block_rows(gcol[c], CHUNK, CHUNK - 1) for c in chains]
    decay = [jnp.exp2(jnp.where(causal, gcol[c] - grow[c], -jnp.inf)) for c in chains]
    e = _unit_lower_inverse_minus_identity(
        [jnp.where(strict, kk[head_of[c]] * bcol[c] * decay[c], 0.0) for c in chains])
    eg = [jnp.exp2(gcol[c]) for c in chains]
    rhs = [jnp.concatenate([v_all[:, lanes[c]] * bcol[c], k[head_of[c]] * (bcol[c] * eg[c])], axis=1)
           for c in chains]
    sol = [rhs[c] + _bdot(e[c], rhs[c]) for c in chains]
    qk = [jnp.where(causal, qk_raw[head_of[c]] * decay[c], 0.0) for c in chains]
    qk_sol = [_bdot(qk[c], sol[c]) for c in chains]
    q_eff = [q[head_of[c]] * eg[c] - qk_sol[c][:, HEAD_DIM:] for c in chains]
    k_dec = [k[head_of[c]] * jnp.exp2(glast[c] - gcol[c]) for c in chains]
    g_chunk = [jnp.exp2(glast[c]) for c in chains]
    state = [state_ref[c] for c in chains]
    for j in range(ts // CHUNK):
        rows = slice(j * CHUNK, (j + 1) * CHUNK)
        kt_sol = [_bdot_tn(k_dec[c][rows], sol[c][rows]) for c in chains]
        prod = [_bdot(jnp.concatenate([kt_sol[c][:, HEAD_DIM:], q_eff[c][rows]], axis=0), state[c])
                for c in chains]
        state = [state[c] * g_chunk[c][j * CHUNK:j * CHUNK + 1] - prod[c][:HEAD_DIM] + kt_sol[c][:, :HEAD_DIM]
                 for c in chains]
        for c in chains:
            o = prod[c][HEAD_DIM:] + qk_sol[c][rows, :HEAD_DIM]
            o = o * lax.rsqrt(jnp.mean(o * o, axis=-1, keepdims=True) + NORM_EPS) * nw
            o_ref[rows, lanes[c]] = (o * z_ref[rows, lanes[c]]).astype(o_ref.dtype)
    for c in chains:
        state_ref[c] = state[c]


GDN_TILE = 128
GDN_HEADS_PER_STEP = 8


def gdn_recurrence(qk, v, z_act, gcum, beta, norm_w, ts=GDN_TILE, nh=GDN_HEADS_PER_STEP):
    s = qk.shape[0]
    key_dim = qk.shape[1] // 2
    value_dim = v.shape[1]
    hq = key_dim // HEAD_DIM
    hv = value_dim // HEAD_DIM
    rep = hv // hq
    nh = min(nh, hq)
    ts = min(ts, s)
    assert hq % nh == 0 and s % ts == 0
    qw = nh * HEAD_DIM
    vw = nh * rep * HEAD_DIM
    gcol = gcum.reshape(s, hq, rep).transpose(1, 0, 2)
    bcol = beta.reshape(s, hq, rep).transpose(1, 0, 2)
    grow = gcum.T.reshape(hq, rep, s)
    k_off = key_dim // qw
    return pl.pallas_call(
        _gdn_kernel,
        grid=(hq // nh, s // ts),
        in_specs=[pl.BlockSpec((ts, qw), lambda h, i: (i, h)),
                  pl.BlockSpec((ts, qw), lambda h, i: (i, k_off + h)),
                  pl.BlockSpec((ts, vw), lambda h, i: (i, h)),
                  pl.BlockSpec((ts, vw), lambda h, i: (i, h)),
                  pl.BlockSpec((nh, ts, rep), lambda h, i: (h, i, 0)),
                  pl.BlockSpec((nh, ts, rep), lambda h, i: (h, i, 0)),
                  pl.BlockSpec((nh, rep, ts), lambda h, i: (h, 0, i)),
                  pl.BlockSpec((1, HEAD_DIM), lambda h, i: (0, 0))],
        out_specs=pl.BlockSpec((ts, vw), lambda h, i: (i, h)),
        out_shape=jax.ShapeDtypeStruct((s, value_dim), BF16),
        scratch_shapes=[pltpu.VMEM((nh * rep, HEAD_DIM, HEAD_DIM), F32)],
        compiler_params=_params("parallel", "arbitrary"),
        name="gdn_recurrence",
    )(qk, qk, v, z_act, gcol, bcol, grow, norm_w.reshape(1, HEAD_DIM))


def gated_deltanet_mixer(u, w_in, conv_w, a_log, dt_bias, norm_w, w_out):
    hv = a_log.shape[0]
    value_dim = hv * HEAD_DIM
    key_dim = (w_in.shape[1] - 2 * value_dim - 2 * hv) // 2
    c_qk, c_v, c_z = 2 * key_dim, 2 * key_dim + value_dim, 2 * key_dim + 2 * value_dim
    q_scale = jnp.concatenate([jnp.full((key_dim,), HEAD_DIM ** -0.5, F32), jnp.ones((key_dim,), F32)])
    qk = proj_act(u, w_in[:, :c_qk].astype(BF16), "conv_norm", conv_w[:, :c_qk], q_scale, name="gdn_qk_proj")
    v = proj_act(u, w_in[:, c_qk:c_v].astype(BF16), "conv", conv_w[:, c_qk:c_v], name="gdn_v_proj")
    z_act = proj_act(u, w_in[:, c_v:c_z].astype(BF16), "silu", name="gdn_z_proj")
    ab = matmul(u, w_in[:, c_z:].astype(BF16), name="gdn_gate_proj")
    gcum, beta = gdn_gates(ab, a_log, dt_bias)
    o = gdn_recurrence(qk, v, z_act, gcum, beta, norm_w)
    return matmul(o, w_out.astype(BF16), tm=512, tn=512, name="gdn_out_proj")


def _block_rows(x, sub, offset):
    n = x.shape[0] // sub
    x3 = x.reshape(n, sub, x.shape[1])
    return jnp.broadcast_to(x3[:, offset:offset + 1, :], x3.shape).reshape(x.shape)


def _gla_masks():
    row = lax.broadcasted_iota(jnp.int32, (CHUNK, CHUNK), 0)
    col = lax.broadcasted_iota(jnp.int32, (CHUNK, CHUNK), 1)
    pair = {}
    size = CHUNK // 2
    while size >= GLA_SUB:
        pair[size] = ((row // size) % 2 == 1) & ((row // size) == (col // size) + 1)
        size //= 2
    same_sub = (row // GLA_SUB) == (col // GLA_SUB)
    row_in = lax.broadcasted_iota(jnp.int32, (CHUNK, 1), 0) % GLA_SUB
    diag_col = [same_sub & ((col % GLA_SUB) == j) for j in range(GLA_SUB)]
    row_from = [row_in >= j for j in range(GLA_SUB)]
    return pair, diag_col, row_from


def _gla_intra_chunk(q, k, b, masks):
    pair, diag_col, row_from = masks
    att = jnp.zeros((CHUNK, CHUNK), F32)
    for size in pair:
        b_row_ref = _block_rows(b, size, 0)
        nxt = jnp.concatenate([b_row_ref[size:], jnp.broadcast_to(b[CHUNK - 1:CHUNK], (size, b.shape[1]))], axis=0)
        qs = q * jnp.exp2(b - b_row_ref)
        ks = k * jnp.exp2(nxt - b)
        att = att + jnp.where(pair[size], _bdot_nt(qs, ks), 0.0)
    for j in range(GLA_SUB):
        bj = _block_rows(b, GLA_SUB, j)
        kj = _block_rows(k, GLA_SUB, j)
        e = jnp.exp2(jnp.where(row_from[j], b - bj, -jnp.inf))
        s = jnp.sum(q * e * kj, axis=-1, keepdims=True)
        att = jnp.where(diag_col[j], s, att)
    return att


def _hgrn_lower_bound(lbl_ref, layer):
    depth = lbl_ref.shape[0]
    rows = [lbl_ref[i:i + 1, :] for i in range(depth)]
    m = functools.reduce(jnp.maximum, rows)
    ex = [jnp.exp(r - m) for r in rows]
    total = functools.reduce(lambda x, y: x + y, ex)
    acc = jnp.zeros_like(m)
    for i in range(1, layer + 1):
        acc = acc + ex[i] / total
    return acc


def _hgrn_kernel(q_ref, f_ref, i_ref, og_ref, lbl_ref, nw_ref, o_ref, state_ref, *, layer):
    ts = q_ref.shape[0]

    @pl.when(pl.program_id(1) == 0)
    def _():
        state_ref[...] = jnp.zeros(state_ref.shape, F32)

    lb = _hgrn_lower_bound(lbl_ref, layer)
    f = f_ref[...]
    log_sig = jnp.minimum(f, 0.0) - jnp.log(1.0 + jnp.exp(-jnp.abs(f)))
    t0 = jnp.log(lb)
    t1 = jnp.log1p(-lb) + log_sig
    log_f = jnp.maximum(t0, t1) + jnp.log(1.0 + jnp.exp(-jnp.abs(t0 - t1)))
    q = _silu(q_ref[...])
    k = (1.0 - lb) * _sigmoid(-f)
    row = lax.broadcasted_iota(jnp.int32, (CHUNK, CHUNK), 0)
    col = lax.broadcasted_iota(jnp.int32, (CHUNK, CHUNK), 1)
    tri = (row >= col).astype(F32)
    nw = nw_ref[...]
    nh = q.shape[1] // HEAD_DIM
    heads = range(nh)
    units = [(h, j) for j in range(ts // CHUNK) for h in heads]
    view = {(h, j): (slice(j * CHUNK, (j + 1) * CHUNK), slice(h * HEAD_DIM, (h + 1) * HEAD_DIM)) for h, j in units}
    log2_f = log_f * LOG2_E
    b = {u: jnp.dot(tri, log2_f[view[u]], preferred_element_type=F32, precision=lax.Precision.HIGHEST)
         for u in units}
    masks = _gla_masks()
    att = {u: _gla_intra_chunk(q[view[u]], k[view[u]], b[u], masks) for u in units}
    o_intra = {u: _bdot(att[u], i_ref[view[u]]) for u in units}
    q_dec = {u: q[view[u]] * jnp.exp2(b[u]) for u in units}
    k_dec = {u: k[view[u]] * jnp.exp2(b[u][CHUNK - 1:CHUNK, :] - b[u]) for u in units}
    kv = {u: _bdot_tn(i_ref[view[u]], k_dec[u]) for u in units}
    state_t = [state_ref[h] for h in heads]
    for j in range(ts // CHUNK):
        o = [o_intra[h, j] + _bdot_nt(q_dec[h, j], state_t[h]) for h in heads]
        state_t = [state_t[h] * jnp.exp2(b[h, j][CHUNK - 1:CHUNK, :]) + kv[h, j] for h in heads]
        for h in heads:
            oh = o[h] * lax.rsqrt(jnp.mean(o[h] * o[h], axis=-1, keepdims=True) + NORM_EPS) * nw
            o_ref[view[h, j]] = (oh * _silu(og_ref[view[h, j]])).astype(o_ref.dtype)
    for h in heads:
        state_ref[h] = state_t[h]


HGRN_TILE = 256
HGRN_HEADS_PER_STEP = 8


def hgrn_recurrence(proj, lb_logits, layer, norm_w, ts=HGRN_TILE, nh=HGRN_HEADS_PER_STEP):
    s, n = proj.shape
    d = n // 4
    heads = d // HEAD_DIM
    ts = min(ts, s)
    nh = min(nh, heads)
    assert heads % nh == 0 and s % ts == 0
    depth = lb_logits.shape[0]
    groups = heads // nh
    width = nh * HEAD_DIM

    def col(off):
        return pl.BlockSpec((ts, width), lambda h, i: (i, off * groups + h))

    return pl.pallas_call(
        functools.partial(_hgrn_kernel, layer=layer),
        grid=(groups, s // ts),
        in_specs=[col(0), col(1), col(2), col(3),
                  pl.BlockSpec((depth, width), lambda h, i: (0, h)),
                  pl.BlockSpec((1, HEAD_DIM), lambda h, i: (0, 0))],
        out_specs=pl.BlockSpec((ts, width), lambda h, i: (i, h)),
        out_shape=jax.ShapeDtypeStruct((s, d), BF16),
        scratch_shapes=[pltpu.VMEM((nh, HEAD_DIM, HEAD_DIM), F32)],
        compiler_params=_params("parallel", "arbitrary"),
        name="hgrn_recurrence",
    )(proj, proj, proj, proj, lb_logits, norm_w.reshape(1, HEAD_DIM))


def hgrn2_mixer(u, w_in, lb_logits, layer, norm_w, w_out):
    proj = matmul(u, w_in.astype(BF16), name="hgrn_in_proj")
    o = hgrn_recurrence(proj, lb_logits, layer, norm_w)
    return matmul(o, w_out.astype(BF16), name="hgrn_out_proj")


ROUTER_LANES = 128
LANE_SENTINEL = 1 << 20


def _router_kernel(h_ref, mod_ref, w_ref, eid_ref, gate_ref, rank_ref, count_ref, run_ref, *,
                   shift_row, scale_row, groups, epg_shift):
    @pl.when(pl.program_id(0) == 0)
    def _():
        run_ref[...] = jnp.zeros(run_ref.shape, F32)

    shift = mod_ref[shift_row:shift_row + 1, :]
    scale = mod_ref[scale_row:scale_row + 1, :]
    u = h_ref[...] * (1.0 + scale) + shift
    logits = jnp.dot(u, w_ref[...], preferred_element_type=F32, precision=lax.Precision.HIGHEST)
    lane = lax.broadcasted_iota(jnp.int32, logits.shape, 1)
    n_exp = groups << epg_shift

    def first_argmax(vals):
        m = jnp.max(vals, axis=-1, keepdims=True)
        idx = jnp.min(jnp.where(vals == m, lane, LANE_SENTINEL), axis=-1, keepdims=True)
        return m, idx

    is_group = lane < groups
    gmax, grp = first_argmax(jnp.where(is_group, logits, -jnp.inf))
    p_grp = 1.0 / jnp.sum(jnp.where(is_group, jnp.exp(logits - gmax), 0.0), axis=-1, keepdims=True)
    e_lane = lane - groups
    in_grp = (e_lane >= 0) & (e_lane < n_exp) & (lax.shift_right_arithmetic(e_lane, epg_shift) == grp)
    el = jnp.where(in_grp, logits, -jnp.inf)
    m1, i1 = first_argmax(el)
    z = jnp.sum(jnp.where(in_grp, jnp.exp(logits - m1), 0.0), axis=-1, keepdims=True)
    m2, i2 = first_argmax(jnp.where(lane == i1, -jnp.inf, el))
    p1 = 1.0 / z
    p2 = jnp.exp(m2 - m1) / z
    denom = p1 + p2
    e1, e2 = i1 - groups, i2 - groups
    eid_ref[...] = jnp.where(lane == 0, e1, jnp.where(lane == 1, e2, 0))
    gate_ref[...] = jnp.where(lane == 0, p_grp * p1 / denom, jnp.where(lane == 1, p_grp * p2 / denom, 0.0))

    tm = logits.shape[0]
    tri = (lax.broadcasted_iota(jnp.int32, (tm, tm), 0) >= lax.broadcasted_iota(jnp.int32, (tm, tm), 1)).astype(BF16)
    hot1 = (lane == e1).astype(F32)
    hot2 = (lane == e2).astype(F32)
    cum1 = jnp.dot(tri, hot1.astype(BF16), preferred_element_type=F32)
    cum2 = jnp.dot(tri, hot2.astype(BF16), preferred_element_type=F32)
    run = run_ref[...]
    after1 = run + cum1[tm - 1:tm, :]
    rank1 = jnp.sum(hot1 * (run + cum1), axis=-1, keepdims=True) - 1.0
    rank2 = jnp.sum(hot2 * (after1 + cum2), axis=-1, keepdims=True) - 1.0
    total = after1 + cum2[tm - 1:tm, :]
    run_ref[...] = total
    count_ref[...] = total.astype(jnp.int32)
    rank_ref[...] = jnp.where(lane == 0, rank1, jnp.where(lane == 1, rank2, 0.0)).astype(jnp.int32)


def moe_router(h, mod, shift_row, scale_row, w_group, w_expert, tm=256):
    s, d = h.shape
    groups = w_group.shape[1]
    n_exp = w_expert.shape[1]
    epg = n_exp // groups
    assert epg & (epg - 1) == 0 and groups + n_exp <= ROUTER_LANES
    w = jnp.concatenate([w_group, w_expert, jnp.zeros((d, ROUTER_LANES - groups - n_exp), F32)], axis=1)
    out = pl.BlockSpec((tm, ROUTER_LANES), lambda i: (i, 0))
    ints = jax.ShapeDtypeStruct((s, ROUTER_LANES), jnp.int32)
    eid, gate, rank, counts = pl.pallas_call(
        functools.partial(_router_kernel, shift_row=shift_row, scale_row=scale_row, groups=groups,
                          epg_shift=epg.bit_length() - 1),
        grid=(s // tm,),
        in_specs=[pl.BlockSpec((tm, d), lambda i: (i, 0)),
                  pl.BlockSpec(mod.shape, lambda i: (0, 0)),
                  pl.BlockSpec((d, ROUTER_LANES), lambda i: (0, 0))],
        out_specs=[out, out, out, pl.BlockSpec((1, ROUTER_LANES), lambda i: (0, 0))],
        out_shape=[ints, jax.ShapeDtypeStruct((s, ROUTER_LANES), F32), ints,
                   jax.ShapeDtypeStruct((1, ROUTER_LANES), jnp.int32)],
        scratch_shapes=[pltpu.VMEM((1, ROUTER_LANES), F32)],
        compiler_params=_params("arbitrary"),
        name="moe_router",
    )(h, mod, w)
    return eid[:, :MOE_TOP_K], gate[:, :MOE_TOP_K], rank[:, :MOE_TOP_K], counts[0, :n_exp]


def _dispatch_plan(eid, rank, counts):
    s = eid.shape[0]
    n_exp = counts.shape[0]
    a = s * MOE_TOP_K
    padded = (counts + MOE_ROWS - 1) // MOE_ROWS * MOE_ROWS
    pad_end = jnp.cumsum(padded)
    pad_start = pad_end - padded
    dest = pad_start[eid] + rank
    n_blk = -(-a // MOE_ROWS) + n_exp
    tok_buf = jnp.zeros((n_blk * MOE_ROWS,), jnp.int32).at[dest.reshape(a)].set(
        jnp.arange(a, dtype=jnp.int32) // MOE_TOP_K)
    blk_start = jnp.arange(n_blk, dtype=jnp.int32) * MOE_ROWS
    blk_e = jnp.minimum(jnp.searchsorted(pad_end, blk_start, side="right"), n_exp - 1).astype(jnp.int32)
    blk_used = (blk_start < pad_end[-1]).astype(jnp.int32)
    return tok_buf, blk_e, blk_used, dest


def _for_rows(n, fn):
    for r in range(n):
        fn(r)


def _moe_ffn_kernel(blk_e_ref, blk_used_ref, tok_cur, tok_next, up_hbm, wg_ref, wu_ref, wd_ref, y_ref,
                    xbuf, sem, w_in_bf, wd_bf, prod_ref):
    f, d = wd_bf.shape
    half = d // 2
    b = pl.program_id(0)
    n = pl.num_programs(0)
    slot = b % 2
    used = blk_used_ref[b] > 0
    next_used = blk_used_ref[jnp.minimum(b + 1, n - 1)] > 0

    def row_copy(tok_ref, to_slot, r):
        return pltpu.make_async_copy(up_hbm.at[tok_ref[0, 0, r]], xbuf.at[to_slot, r], sem.at[to_slot])

    @pl.when((b == 0) & used)
    def _():
        _for_rows(MOE_ROWS, lambda r: row_copy(tok_cur, 0, r).start(priority=r % DMA_QUEUES))

    @pl.when((b + 1 < n) & next_used)
    def _():
        _for_rows(MOE_ROWS, lambda r: row_copy(tok_next, 1 - slot, r).start(priority=r % DMA_QUEUES))

    @pl.when(used & ((b == 0) | (blk_e_ref[b] != blk_e_ref[jnp.maximum(b - 1, 0)])))
    def _():
        w_in_bf[:, 0 * f:1 * f] = wg_ref[0, 0, 0:half, :].astype(BF16)
        w_in_bf[:, 1 * f:2 * f] = wg_ref[0, 0, half:d, :].astype(BF16)
        w_in_bf[:, 2 * f:3 * f] = wu_ref[0, 0, 0:half, :].astype(BF16)
        w_in_bf[:, 3 * f:4 * f] = wu_ref[0, 0, half:d, :].astype(BF16)
        wd_bf[...] = wd_ref[0, 0].astype(BF16)

    @pl.when(used)
    def _():
        _for_rows(MOE_ROWS, lambda r: row_copy(tok_cur, slot, r).wait())
        x2 = jnp.concatenate([pltpu.bitcast(xbuf[slot, :, j, :], BF16) for j in range(xbuf.shape[2])], axis=1)
        prod = jnp.dot(x2, w_in_bf[...], preferred_element_type=F32)
        tiles = f // HEAD_DIM
        for t in range(4 * tiles):
            prod_ref[t] = prod[:, t * HEAD_DIM:(t + 1) * HEAD_DIM]

        def rows(parity, first_tile):
            return jnp.concatenate([prod_ref[first_tile + t, pl.ds(parity, MOE_ROWS, stride=2), :]
                                    for t in range(tiles)], axis=1)

        g = rows(0, 0) + rows(1, tiles)
        up = rows(0, 2 * tiles) + rows(1, 3 * tiles)
        hid = (_silu(g) * up).astype(BF16)
        y = jnp.dot(hid, wd_bf[...], preferred_element_type=F32)
        for j in range(y_ref.shape[1]):
            y_ref[:, j, :] = y[:, j * HEAD_DIM:(j + 1) * HEAD_DIM]

    @pl.when(jnp.logical_not(used))
    def _():
        y_ref[...] = jnp.zeros(y_ref.shape, F32)


def moe_ffn(up, tok_buf, blk_e, blk_used, w_gate, w_up, w_down, layer):
    p = tok_buf.shape[0]
    d, f = w_gate.shape[-2:]
    n_blk = p // MOE_ROWS
    slabs_in = up.shape[1]
    tok3 = tok_buf.reshape(n_blk, 1, MOE_ROWS)
    grid_spec = pltpu.PrefetchScalarGridSpec(
        num_scalar_prefetch=2,
        grid=(n_blk,),
        in_specs=[pl.BlockSpec((1, 1, MOE_ROWS), lambda b, e, u: (b, 0, 0), memory_space=pltpu.SMEM),
                  pl.BlockSpec((1, 1, MOE_ROWS), lambda b, e, u: (jnp.minimum(b + 1, n_blk - 1), 0, 0),
                               memory_space=pltpu.SMEM),
                  pl.BlockSpec(memory_space=pl.ANY),
                  pl.BlockSpec((1, 1, d, f), lambda b, e, u: (layer, e[b], 0, 0)),
                  pl.BlockSpec((1, 1, d, f), lambda b, e, u: (layer, e[b], 0, 0)),
                  pl.BlockSpec((1, 1, f, d), lambda b, e, u: (layer, e[b], 0, 0))],
        out_specs=pl.BlockSpec((MOE_ROWS, d // HEAD_DIM, HEAD_DIM), lambda b, e, u: (b, 0, 0)),
        scratch_shapes=[pltpu.VMEM((2, MOE_ROWS, slabs_in, HEAD_DIM), jnp.uint32), pltpu.SemaphoreType.DMA((2,)),
                        pltpu.VMEM((d // 2, 4 * f), BF16), pltpu.VMEM((f, d), BF16),
                        pltpu.VMEM((4 * f // HEAD_DIM, 2 * MOE_ROWS, HEAD_DIM), F32)],
    )
    return pl.pallas_call(
        _moe_ffn_kernel,
        grid_spec=grid_spec,
        out_shape=jax.ShapeDtypeStruct((p, d // HEAD_DIM, HEAD_DIM), F32),
        compiler_params=_params("arbitrary"),
        name="moe_ffn",
    )(blk_e, blk_used, tok3, tok3, up, w_gate, w_up, w_down)


def _combine_kernel(pos_cur, pos_next, h_ref, gw_ref, mod_ref, lnw_ref, lnb_ref, ys_hbm, *rest,
                    gate_row, shift_row, scale_row, alpha):
    out_refs, (buf, sem) = rest[:-2], rest[-2:]
    tm = h_ref.shape[0]
    i = pl.program_id(0)
    n = pl.num_programs(0)
    slot = i % 2

    def row_copy(pos_ref, to_slot, k, t):
        return pltpu.make_async_copy(ys_hbm.at[pos_ref[0, 0, k * tm + t]], buf.at[to_slot, k, t], sem.at[to_slot])

    def for_all_rows(fn):
        for k in range(MOE_TOP_K):
            _for_rows(tm, functools.partial(fn, k))

    @pl.when(i == 0)
    def _():
        for_all_rows(lambda k, t: row_copy(pos_cur, 0, k, t).start(priority=t % DMA_QUEUES))

    @pl.when(i + 1 < n)
    def _():
        for_all_rows(lambda k, t: row_copy(pos_next, 1 - slot, k, t).start(priority=t % DMA_QUEUES))

    for_all_rows(lambda k, t: row_copy(pos_cur, slot, k, t).wait())

    pieces = []
    for j in range(buf.shape[3]):
        piece = gw_ref[:, 0:1] * buf[slot, 0, :, j, :]
        for k in range(1, MOE_TOP_K):
            piece = piece + gw_ref[:, k:k + 1] * buf[slot, k, :, j, :]
        pieces.append(piece)
    y = jnp.concatenate(pieces, axis=1)
    gate = mod_ref[gate_row:gate_row + 1, :]
    hn = _residual_norm(h_ref[...], y, gate, lnw_ref[...], lnb_ref[...], alpha)
    out_refs[0][...] = hn
    if shift_row is not None:
        shift = mod_ref[shift_row:shift_row + 1, :]
        scale = mod_ref[scale_row:scale_row + 1, :]
        out_refs[1][...] = (hn * (1.0 + scale) + shift).astype(BF16)


def moe_combine_norm(h, ys, pos, gate_w, mod, lnw, lnb, gate_row, shift_row, scale_row, alpha, tm=128):
    s, d = h.shape
    tm = min(tm, s)
    n = s // tm
    pos_tiles = pos.reshape(n, tm, MOE_TOP_K).transpose(0, 2, 1).reshape(n, 1, MOE_TOP_K * tm)
    row = pl.BlockSpec((tm, d), lambda i: (i, 0))
    vec = pl.BlockSpec((1, d), lambda i: (0, 0))
    with_u = shift_row is not None
    out_shape = [jax.ShapeDtypeStruct((s, d), F32)] + ([jax.ShapeDtypeStruct((s, d), BF16)] if with_u else [])
    outs = pl.pallas_call(
        functools.partial(_combine_kernel, gate_row=gate_row, shift_row=shift_row, scale_row=scale_row, alpha=alpha),
        grid=(n,),
        in_specs=[pl.BlockSpec((1, 1, MOE_TOP_K * tm), lambda i: (i, 0, 0), memory_space=pltpu.SMEM),
                  pl.BlockSpec((1, 1, MOE_TOP_K * tm), lambda i: (jnp.minimum(i + 1, n - 1), 0, 0),
                               memory_space=pltpu.SMEM),
                  row,
                  pl.BlockSpec((tm, MOE_TOP_K), lambda i: (i, 0)),
                  pl.BlockSpec(mod.shape, lambda i: (0, 0)), vec, vec,
                  pl.BlockSpec(memory_space=pl.ANY)],
        out_specs=[row] * len(out_shape),
        out_shape=out_shape,
        scratch_shapes=[pltpu.VMEM((2, MOE_TOP_K, tm) + ys.shape[1:], F32), pltpu.SemaphoreType.DMA((2,))],
        compiler_params=_params("arbitrary"),
        name="moe_combine_norm",
    )(pos_tiles, pos_tiles, h, gate_w, mod, lnw.reshape(1, d), lnb.reshape(1, d), ys)
    return (outs[0], outs[1]) if with_u else (outs[0], None)


def moe_sublayer(h, up, mod, lnw, lnb, w_group, w_expert, w_gate, w_up, w_down, layer,
                 cur_shift_row, cur_scale_row, gate_row, shift_row, scale_row, alpha):
    eid, gate_w, rank, counts = moe_router(h, mod, cur_shift_row, cur_scale_row, w_group, w_expert)
    tok_buf, blk_e, blk_used, pos = _dispatch_plan(eid, rank, counts)
    ys = moe_ffn(up, tok_buf, blk_e, blk_used, w_gate, w_up, w_down, layer)
    return moe_combine_norm(h, ys, pos, gate_w, mod, lnw, lnb, gate_row, shift_row, scale_row, alpha)


def kernel(x, c, ada_w, ada_b, ln_w, ln_b, gdn_w_in, gdn_conv_w, gdn_a_log, gdn_dt_bias, gdn_norm_w, gdn_w_out,
           hgrn_w_in, hgrn_lb_logits, hgrn_norm_w, hgrn_w_out, moe_w_group, moe_w_expert, moe_w_gate, moe_w_up,
           moe_w_down):
    batch, s, d = x.shape
    assert batch == 1, "one sequence per call"
    depth = ada_w.shape[0]
    alpha = (2 * depth) ** 0.25
    mod = ada_modulation(c, ada_w, ada_b)
    h = x.reshape(s, d)
    u = modulate(h, mod, 0, 1)
    for layer in range(depth):
        base = 6 * layer
        j = layer // 2
        if layer % 2 == 0:
            y = gated_deltanet_mixer(u, gdn_w_in[j], gdn_conv_w[j], gdn_a_log[j], gdn_dt_bias[j], gdn_norm_w[j],
                                     gdn_w_out[j])
        else:
            y = hgrn2_mixer(u, hgrn_w_in[j], hgrn_lb_logits, layer, hgrn_norm_w[j], hgrn_w_out[j])
        h, u = residual_norm(h, y, mod, ln_w[layer, 0], ln_b[layer, 0], base + 2, base + 3, base + 4, alpha)
        last = layer + 1 == depth
        h, u = moe_sublayer(h, u, mod, ln_w[layer, 1], ln_b[layer, 1], moe_w_group[layer], moe_w_expert[layer],
                            moe_w_gate, moe_w_up, moe_w_down, layer,
                            cur_shift_row=base + 3, cur_scale_row=base + 4, gate_row=base + 5,
                            shift_row=None if last else base + 6, scale_row=None if last else base + 7, alpha=alpha)
    return h.reshape(batch, s, d)
```
